```python
import jax, jax.numpy as jnp
from jax import lax
import numpy as np

D_MODEL = 1024
BATCH = 4
SEQ = 4096
DEPTH = 2

HEAD_DIM = 64
N_RET_HEADS = 6
N_GM_HEADS = 4
N_FOX_HEADS = 6
D_RET = N_RET_HEADS * HEAD_DIM
D_GM = N_GM_HEADS * HEAD_DIM
D_FOX = N_FOX_HEADS * HEAD_DIM
D_MIX = D_RET + D_GM + D_FOX
CHUNK = 128
D_FF = 2816
CONV_W = 3
ROPE_BASE = 10000.0
EPS = 1e-6
NEG_INF = -1e30
SPLIT_SIZES = (D_RET, D_RET, D_RET, D_RET, D_GM, D_GM, D_FOX, D_FOX, D_FOX, N_FOX_HEADS)
D_IN = sum(SPLIT_SIZES)

kernel_name = "hybrid_retention_gmlp_fox_block"


def split_points():
    pts, acc = [], 0
    for s in SPLIT_SIZES[:-1]:
        acc += s
        pts.append(acc)
    return pts


def rms_norm(x, g):
    xf = x.astype(jnp.float32)
    y = xf * lax.rsqrt(jnp.mean(xf * xf, axis=-1, keepdims=True) + EPS)
    return (y * g.astype(jnp.float32)).astype(x.dtype)


def layer_norm(x, g, b):
    xf = x.astype(jnp.float32)
    mu = jnp.mean(xf, axis=-1, keepdims=True)
    var = jnp.mean(jnp.square(xf - mu), axis=-1, keepdims=True)
    return (xf - mu) * lax.rsqrt(var + EPS) * g.astype(jnp.float32) + b.astype(jnp.float32)


def rotary(x):
    S = x.shape[1]
    half = HEAD_DIM // 2
    inv = ROPE_BASE ** (-jnp.arange(half, dtype=jnp.float32) / half)
    ang = jnp.arange(S, dtype=jnp.float32)[:, None] * inv[None, :]
    cos = jnp.cos(ang)[None, :, None, :]
    sin = jnp.sin(ang)[None, :, None, :]
    x1, x2 = x[..., :half], x[..., half:]
    return jnp.concatenate([x1 * cos - x2 * sin, x1 * sin + x2 * cos], axis=-1)


def retention(q, k, v):
    B, S, H, D = q.shape
    N = S // CHUNK
    gamma = 1.0 - 2.0 ** (-5.0 - jnp.arange(H, dtype=jnp.float32))
    log_g = jnp.log(gamma)
    qc = q.reshape(B, N, CHUNK, H, D)
    kc = k.reshape(B, N, CHUNK, H, D)
    vc = v.reshape(B, N, CHUNK, H, D)
    idx = jnp.arange(CHUNK, dtype=jnp.float32)
    rel = idx[:, None] - idx[None, :]
    decay = jnp.where(rel[None] >= 0,
                      jnp.exp(log_g[:, None, None] * jnp.maximum(rel, 0.0)[None]), 0.0)
    scores = jnp.einsum('bnthd,bnshd->bnhts', qc, kc) * decay[None, None]
    y_inner = jnp.einsum('bnhts,bnshe->bnthe', scores, vc)
    k_dec = jnp.exp(log_g[None, :] * (CHUNK - 1 - idx)[:, None])
    kv = jnp.einsum('bnshd,bnshe->bnhde', kc * k_dec[None, None, :, :, None], vc)
    chunk_decay = jnp.exp(log_g * CHUNK)[None, :, None, None]

    def step(state, kv_n):
        return state * chunk_decay + kv_n, state

    init = jnp.zeros((B, H, D, D), jnp.float32)
    _, prev = lax.scan(step, init, jnp.moveaxis(kv, 1, 0))
    prev = jnp.moveaxis(prev, 0, 1)
    q_dec = jnp.exp(log_g[None, :] * (idx + 1.0)[:, None])
    y_cross = jnp.einsum('bnthd,bnhde->bnthe', qc * q_dec[None, None, :, :, None], prev)
    return (y_inner + y_cross).reshape(B, S, H, D)


def chunk_spatial_gate(u, v, w_s, b_s):
    B, S, G, D = v.shape
    N = S // CHUNK
    mask = jnp.tril(jnp.ones((CHUNK, CHUNK), dtype=bool))
    w = jnp.where(mask[None], w_s, 0.0)
    vc = v.reshape(B, N, CHUNK, G, D)
    mixed = jnp.einsum('gts,bnsgd->bntgd', w, vc) + b_s.T[None, None, :, :, None]
    return u * mixed.reshape(B, S, G, D)


def forgetting_attention(q, k, v, log_f):
    B, S, H, D = q.shape
    N = S // CHUNK
    scale = D ** -0.5
    cum = jnp.transpose(jnp.cumsum(log_f, axis=1), (0, 2, 1))
    kpos = jnp.arange(S)

    def block(i):
        start = i * CHUNK
        qs = lax.dynamic_slice_in_dim(q, start, CHUNK, axis=1)
        cq = lax.dynamic_slice_in_dim(cum, start, CHUNK, axis=2)
        logits = jnp.einsum('bthd,bshd->bhts', qs, k) * scale
        logits = logits + (cq[..., :, None] - cum[..., None, :])
        qpos = start + jnp.arange(CHUNK)
        mask = kpos[None, :] <= qpos[:, None]
        logits = jnp.where(mask[None, None], logits, NEG_INF)
        p = jax.nn.softmax(logits, axis=-1)
        return jnp.einsum('bhts,bshd->bthd', p, v)

    out = lax.map(block, jnp.arange(N))
    return jnp.moveaxis(out, 0, 1).reshape(B, S, H, D)


def causal_dwconv(h, w, b):
    C = h.shape[-1]
    y = lax.conv_general_dilated(h, w[:, None, :], window_strides=(1,),
                                 padding=((CONV_W - 1, 0),),
                                 dimension_numbers=('NWC', 'WIO', 'NWC'),
                                 feature_group_count=C)
    return y + b


def setup_inputs(seed: int = 0) -> dict:
    key = jax.random.key(seed)
    ks = jax.random.split(key, 20)
    L, D = DEPTH, D_MODEL

    def nrm(k, shape, s):
        return jax.random.normal(k, shape, jnp.float32) * s

    return {
        "x": nrm(ks[0], (BATCH, SEQ, D), 1.0),
        "c": nrm(ks[1], (BATCH, D), 1.0),
        "ada_w": nrm(ks[2], (L, D, 6 * D), 0.5 * D ** -0.5),
        "ada_b": nrm(ks[3], (L, 6 * D), 0.02),
        "norm1_g": 1.0 + nrm(ks[4], (L, D), 0.02),
        "w_in": nrm(ks[5], (L, D, D_IN), D ** -0.5),
        "ret_norm_g": 1.0 + nrm(ks[6], (L, N_RET_HEADS, HEAD_DIM), 0.02),
        "gm_ln_g": 1.0 + nrm(ks[7], (L, D_GM), 0.02),
        "gm_ln_b": nrm(ks[8], (L, D_GM), 0.02),
        "gm_ws": nrm(ks[9], (L, N_GM_HEADS, CHUNK, CHUNK), CHUNK ** -0.5),
        "gm_bs": 1.0 + nrm(ks[10], (L, N_GM_HEADS, CHUNK), 0.02),
        "fox_qn_g": 1.0 + nrm(ks[11], (L, HEAD_DIM), 0.02),
        "fox_kn_g": 1.0 + nrm(ks[12], (L, HEAD_DIM), 0.02),
        "fox_bf": jax.random.uniform(ks[13], (L, N_FOX_HEADS), jnp.float32, 2.0, 6.0),
        "w_o": nrm(ks[14], (L, D_MIX, D), D_MIX ** -0.5),
        "norm2_g": 1.0 + nrm(ks[15], (L, D), 0.02),
        "w_up": nrm(ks[16], (L, D, 2 * D_FF), D ** -0.5),
        "conv_w": nrm(ks[17], (L, CONV_W, 2 * D_FF), CONV_W ** -0.5),
        "conv_b": nrm(ks[18], (L, 2 * D_FF), 0.02),
        "w_down": nrm(ks[19], (L, D_FF, D), D_FF ** -0.5),
    }


def reference(x, c, ada_w, ada_b, norm1_g, w_in, ret_norm_g, gm_ln_g, gm_ln_b, gm_ws, gm_bs,
              fox_qn_g, fox_kn_g, fox_bf, w_o, norm2_g, w_up, conv_w, conv_b, w_down):
    B, S, _ = x.shape
    f32 = jnp.float32
    cond = jax.nn.silu(c)
    for l in range(DEPTH):
        mod = cond @ ada_w[l] + ada_b[l]
        sh1, sc1, gt1, sh2, sc2, gt2 = [m[:, None, :] for m in jnp.split(mod, 6, axis=-1)]

        h = rms_norm(x, norm1_g[l]) * (1.0 + sc1) + sh1
        proj = h @ w_in[l]
        q_r, k_r, v_r, g_r, u_g, v_g, q_f, k_f, v_f, f_f = jnp.split(proj, split_points(), axis=-1)

        qr = rotary(q_r.reshape(B, S, N_RET_HEADS, HEAD_DIM).astype(f32))
        kr = rotary(k_r.reshape(B, S, N_RET_HEADS, HEAD_DIM).astype(f32)) * (HEAD_DIM ** -0.5)
        vr = v_r.reshape(B, S, N_RET_HEADS, HEAD_DIM).astype(f32)
        y_r = rms_norm(retention(qr, kr, vr), ret_norm_g[l]).reshape(B, S, D_RET)
        y_r = y_r * jax.nn.silu(g_r.astype(f32))

        u = jax.nn.gelu(u_g.astype(f32))
        vg = layer_norm(jax.nn.gelu(v_g.astype(f32)), gm_ln_g[l], gm_ln_b[l])
        y_g = chunk_spatial_gate(u.reshape(B, S, N_GM_HEADS, HEAD_DIM),
                                 vg.reshape(B, S, N_GM_HEADS, HEAD_DIM),
                                 gm_ws[l].astype(f32), gm_bs[l].astype(f32)).reshape(B, S, D_GM)

        qf = rms_norm(q_f.reshape(B, S, N_FOX_HEADS, HEAD_DIM).astype(f32), fox_qn_g[l])
        kf = rms_norm(k_f.reshape(B, S, N_FOX_HEADS, HEAD_DIM).astype(f32), fox_kn_g[l])
        vf = v_f.reshape(B, S, N_FOX_HEADS, HEAD_DIM).astype(f32)
        log_f = jax.nn.log_sigmoid(f_f.astype(f32) + fox_bf[l].astype(f32))
        y_f = forgetting_attention(qf, kf, vf, log_f).reshape(B, S, D_FOX)

        mix = jnp.concatenate([y_r, y_g, y_f], axis=-1).astype(x.dtype)
        x = x + gt1 * (mix @ w_o[l])

        h2 = rms_norm(x, norm2_g[l]) * (1.0 + sc2) + sh2
        up = causal_dwconv(h2 @ w_up[l], conv_w[l], conv_b[l])
        a, bv = jnp.split(up, 2, axis=-1)
        x = x + gt2 * ((jax.nn.silu(a) * bv) @ w_down[l])
    return x
```

```python
import functools
import math

import jax
import jax.numpy as jnp
from jax import lax
from jax.experimental import pallas as pl
from jax.experimental.pallas import tpu as pltpu

D_MODEL = 1024
HEAD_DIM = 64
N_RET_HEADS = 6
N_GM_HEADS = 4
N_FOX_HEADS = 6
D_RET = N_RET_HEADS * HEAD_DIM
D_GM = N_GM_HEADS * HEAD_DIM
D_FOX = N_FOX_HEADS * HEAD_DIM
CHUNK = 128
D_FF = 2816
CONV_W = 3
ROPE_BASE = 10000.0
EPS = 1e-6
NEG_INF = -1e30

LANES = 128
SUBLANES = 8
MXU_DIM = 256
VMEM_LIMIT = 52 * 1024 * 1024

N_P384 = 7 * D_RET
N_PGM = 2 * D_GM
N_PROJ = N_P384 + N_PGM + LANES

F32 = jnp.float32
BF16 = jnp.bfloat16


def _cparams(*sem):
    return pltpu.CompilerParams(dimension_semantics=sem, vmem_limit_bytes=VMEM_LIMIT)


def _resident(shape):
    nd = len(shape)
    return pl.BlockSpec(shape, lambda *_: (0,) * nd, pipeline_mode=pl.Buffered(1))


def _lane_iota(shape):
    return lax.broadcasted_iota(jnp.int32, shape, len(shape) - 1)


def _mod_kernel(c_ref, w_ref, b_ref, o_ref):
    c = c_ref[...]
    cond = c * jax.nn.sigmoid(c)
    o_ref[0] = jnp.dot(cond, w_ref[0], precision=lax.Precision.HIGHEST,
                       preferred_element_type=F32) + b_ref[0]


def _modulation(c, ada_w, ada_b):
    L, D, N = ada_w.shape
    B = c.shape[0]
    tn = 1536
    return pl.pallas_call(
        _mod_kernel,
        out_shape=jax.ShapeDtypeStruct((L, B, N), F32),
        grid=(L, N // tn),
        in_specs=[pl.BlockSpec((B, D), lambda l, j: (0, 0)),
                  pl.BlockSpec((1, D, tn), lambda l, j: (l, 0, j)),
                  pl.BlockSpec((1, 1, tn), lambda l, j: (l, 0, j))],
        out_specs=pl.BlockSpec((1, B, tn), lambda l, j: (l, 0, j)),
        compiler_params=_cparams("arbitrary", "arbitrary"),
        name="adaln_mod",
    )(c, ada_w, ada_b.reshape(L, 1, N))


def _rope_kernel(inv_ref, cos_ref, sin_ref):
    ts = cos_ref.shape[0]
    pos = (lax.broadcasted_iota(jnp.int32, (ts, LANES), 0) + pl.program_id(0) * ts).astype(F32)
    ang = pos * inv_ref[...]
    first_half = (_lane_iota((ts, LANES)) % HEAD_DIM) < (HEAD_DIM // 2)
    cos_ref[...] = jnp.cos(ang)
    s = jnp.sin(ang)
    sin_ref[...] = jnp.where(first_half, -s, s)


def _rope_tables(S):
    half = HEAD_DIM // 2
    inv = ROPE_BASE ** (-jnp.arange(half, dtype=F32) / half)
    inv_row = jnp.tile(inv, LANES // half).reshape(1, LANES)
    ts = min(S, 512)
    return pl.pallas_call(
        _rope_kernel,
        out_shape=(jax.ShapeDtypeStruct((S, LANES), F32),) * 2,
        grid=(S // ts,),
        in_specs=[pl.BlockSpec((1, LANES), lambda i: (0, 0))],
        out_specs=(pl.BlockSpec((ts, LANES), lambda i: (i, 0)),) * 2,
        compiler_params=_cparams("arbitrary"),
        name="rope_tables",
    )(inv_row)


def _mod_rms(x, g, sc, sh):
    ms = jnp.mean(x * x, axis=-1, keepdims=True)
    return (x * lax.rsqrt(ms + EPS) * g) * (1.0 + sc) + sh


def _inproj_kernel(x_ref, sc_ref, sh_ref, g_ref, w_ref, p384_ref, pgm_ref, fl_ref):
    h = _mod_rms(x_ref[0], g_ref[...], sc_ref[0], sh_ref[0]).astype(BF16)
    for j in range(N_PROJ // MXU_DIM):
        lo = j * MXU_DIM
        r = jnp.dot(h, w_ref[:, lo:lo + MXU_DIM], preferred_element_type=F32)
        for half in range(MXU_DIM // LANES):
            c0 = lo + half * LANES
            piece = r[:, half * LANES:(half + 1) * LANES]
            if c0 < N_P384:
                p384_ref[0, :, c0:c0 + LANES] = piece.astype(BF16)
            elif c0 < N_P384 + N_PGM:
                pgm_ref[0, :, c0 - N_P384:c0 - N_P384 + LANES] = piece.astype(BF16)
            else:
                fl_ref[0] = piece


def _in_projection(x, sc, sh, g, w_perm, tm):
    B, S, D = x.shape
    return pl.pallas_call(
        _inproj_kernel,
        out_shape=(jax.ShapeDtypeStruct((B, S, N_P384), BF16),
                   jax.ShapeDtypeStruct((B, S, N_PGM), BF16),
                   jax.ShapeDtypeStruct((B, S, LANES), F32)),
        grid=(B, S // tm),
        in_specs=[pl.BlockSpec((1, tm, D), lambda b, i: (b, i, 0)),
                  pl.BlockSpec((1, 1, D), lambda b, i: (b, 0, 0)),
                  pl.BlockSpec((1, 1, D), lambda b, i: (b, 0, 0)),
                  pl.BlockSpec((1, D), lambda b, i: (0, 0)),
                  _resident((D, N_PROJ))],
        out_specs=(pl.BlockSpec((1, tm, N_P384), lambda b, i: (b, i, 0)),
                   pl.BlockSpec((1, tm, N_PGM), lambda b, i: (b, i, 0)),
                   pl.BlockSpec((1, tm, LANES), lambda b, i: (b, i, 0))),
        compiler_params=_cparams("arbitrary", "arbitrary"),
        name="in_proj",
    )(x, sc, sh, g, w_perm)


def _pair_rms_scale(x):
    low = _lane_iota(x.shape) < HEAD_DIM
    sq = x * x
    ss0 = jnp.sum(jnp.where(low, sq, 0.0), axis=-1, keepdims=True)
    ss1 = jnp.sum(jnp.where(low, 0.0, sq), axis=-1, keepdims=True)
    return jnp.where(low, lax.rsqrt(ss0 * (1.0 / HEAD_DIM) + EPS),
                     lax.rsqrt(ss1 * (1.0 / HEAD_DIM) + EPS))


def _swap_half_heads(x):
    first_half = (_lane_iota(x.shape) % HEAD_DIM) < (HEAD_DIM // 2)
    return jnp.where(first_half, pltpu.roll(x, LANES - HEAD_DIM // 2, 1),
                     pltpu.roll(x, HEAD_DIM // 2, 1))


def _ret_log_gamma(h):
    return math.log(1.0 - 2.0 ** (-5.0 - h))


def _ret_kernel(q_ref, k_ref, v_ref, g_ref, cos_ref, sin_ref, ng_ref, o_ref,
                state_ref, decay_ref, qdec_ref, kdec_ref):
    tr = q_ref.shape[1]
    n_pairs = N_RET_HEADS // 2
    low = _lane_iota((CHUNK, LANES)) < HEAD_DIM

    @pl.when(pl.program_id(1) == 0)
    def _():
        state_ref[...] = jnp.zeros_like(state_ref)

    row = lax.broadcasted_iota(jnp.int32, (CHUNK, CHUNK), 0)
    col = lax.broadcasted_iota(jnp.int32, (CHUNK, CHUNK), 1)
    rel = (row - col).astype(F32)
    idx = lax.broadcasted_iota(jnp.int32, (CHUNK, LANES), 0).astype(F32)
    for h in range(N_RET_HEADS):
        lg = _ret_log_gamma(h)
        decay_ref[h] = jnp.where(rel >= 0, jnp.exp(lg * jnp.maximum(rel, 0.0)), 0.0)
    for p in range(n_pairs):
        lg0, lg1 = _ret_log_gamma(2 * p), _ret_log_gamma(2 * p + 1)
        lgl = jnp.where(low, lg0, lg1)
        qdec_ref[p] = jnp.exp(lgl * (idx + 1.0))
        kdec_ref[p] = jnp.exp(lgl * (CHUNK - 1.0 - idx))

    blockdiag = (lax.broadcasted_iota(jnp.int32, (LANES, LANES), 0) < HEAD_DIM) == \
                (lax.broadcasted_iota(jnp.int32, (LANES, LANES), 1) < HEAD_DIM)
    dn_t = (((1,), (1,)), ((), ()))
    dn_ta = (((0,), (0,)), ((), ()))

    for p in range(n_pairs):
        cs = slice(p * LANES, (p + 1) * LANES)
        lg0, lg1 = _ret_log_gamma(2 * p), _ret_log_gamma(2 * p + 1)
        cd = jnp.where(low, math.exp(lg0 * CHUNK), math.exp(lg1 * CHUNK))[0:1, :]
        for c in range(tr // CHUNK):
            rs = slice(c * CHUNK, (c + 1) * CHUNK)
            cos, sin = cos_ref[rs, :], sin_ref[rs, :]
            q = q_ref[0, rs, cs].astype(F32)
            k = k_ref[0, rs, cs].astype(F32)
            q = q * cos + _swap_half_heads(q) * sin
            k = (k * cos + _swap_half_heads(k) * sin) * (HEAD_DIM ** -0.5)
            v = v_ref[0, rs, cs]
            kb = k.astype(BF16)
            q0 = jnp.where(low, q, 0.0).astype(BF16)
            q1 = jnp.where(low, 0.0, q).astype(BF16)
            s0 = lax.dot_general(q0, kb, dn_t, preferred_element_type=F32) * decay_ref[2 * p]
            s1 = lax.dot_general(q1, kb, dn_t, preferred_element_type=F32) * decay_ref[2 * p + 1]
            y0 = jnp.dot(s0.astype(BF16), v, preferred_element_type=F32)
            y1 = jnp.dot(s1.astype(BF16), v, preferred_element_type=F32)
            state = state_ref[p]
            y_cross = jnp.dot((q * qdec_ref[p]).astype(BF16), state.astype(BF16),
                              preferred_element_type=F32)
            y = jnp.where(low, y0, y1) + y_cross
            kv = lax.dot_general((k * kdec_ref[p]).astype(BF16), v, dn_ta,
                                 preferred_element_type=F32)
            state_ref[p] = state * cd + jnp.where(blockdiag, kv, 0.0)
            yn = y * _pair_rms_scale(y) * ng_ref[:, cs]
            gate = g_ref[0, rs, cs].astype(F32)
            o_ref[0, rs, cs] = (yn * (gate * jax.nn.sigmoid(gate))).astype(BF16)


def _retention(p384, cos_t, sin_t, ng, tr):
    B, S, _ = p384.shape
    col = lambda j: pl.BlockSpec((1, tr, D_RET), lambda b, i, j=j: (b, i, j))
    return pl.pallas_call(
        _ret_kernel,
        out_shape=jax.ShapeDtypeStruct((B, S, D_RET), BF16),
        grid=(B, S // tr),
        in_specs=[col(0), col(1), col(2), col(3),
                  pl.BlockSpec((tr, LANES), lambda b, i: (i, 0)),
                  pl.BlockSpec((tr, LANES), lambda b, i: (i, 0)),
                  pl.BlockSpec((1, D_RET), lambda b, i: (0, 0))],
        out_specs=pl.BlockSpec((1, tr, D_RET), lambda b, i: (b, i, 0)),
        scratch_shapes=[pltpu.VMEM((N_RET_HEADS // 2, LANES, LANES), F32),
                        pltpu.VMEM((N_RET_HEADS, CHUNK, CHUNK), F32),
                        pltpu.VMEM((N_RET_HEADS // 2, CHUNK, LANES), F32),
                        pltpu.VMEM((N_RET_HEADS // 2, CHUNK, LANES), F32)],
        compiler_params=_cparams("arbitrary", "arbitrary"),
        name="retention",
    )(p384, p384, p384, p384, cos_t, sin_t, ng)


def _gelu(x):
    return 0.5 * x * (1.0 + jnp.tanh(math.sqrt(2.0 / math.pi) * (x + 0.044715 * (x * x * x))))


def _gmlp_kernel(uv_ref, lg_ref, lb_ref, ws_ref, bias_ref, o_ref):
    tg = uv_ref.shape[1]
    u = _gelu(uv_ref[0, :, :D_GM].astype(F32))
    v = _gelu(uv_ref[0, :, D_GM:].astype(F32))
    mu = jnp.mean(v, axis=-1, keepdims=True)
    vc = v - mu
    var = jnp.mean(vc * vc, axis=-1, keepdims=True)
    vn = (vc * lax.rsqrt(var + EPS) * lg_ref[...] + lb_ref[...]).astype(BF16)
    r_i = lax.broadcasted_iota(jnp.int32, (N_GM_HEADS * CHUNK, CHUNK), 0) % CHUNK
    c_i = lax.broadcasted_iota(jnp.int32, (N_GM_HEADS * CHUNK, CHUNK), 1)
    w = jnp.where(c_i <= r_i, ws_ref[...], 0.0).astype(BF16)
    group = _lane_iota((CHUNK, D_GM)) // HEAD_DIM
    for c in range(tg // CHUNK):
        rs = slice(c * CHUNK, (c + 1) * CHUNK)
        r = jnp.dot(w, vn[rs], preferred_element_type=F32)
        mixed = r[0:CHUNK]
        for g in range(1, N_GM_HEADS):
            mixed = jnp.where(group == g, r[g * CHUNK:(g + 1) * CHUNK], mixed)
        o_ref[0, rs, :] = (u[rs] * (mixed + bias_ref[...])).astype(BF16)


def _spatial_gate(pgm, ln_g, ln_b, ws, bias_tile, tg):
    B, S, _ = pgm.shape
    return pl.pallas_call(
        _gmlp_kernel,
        out_shape=jax.ShapeDtypeStruct((B, S, D_GM), BF16),
        grid=(B, S // tg),
        in_specs=[pl.BlockSpec((1, tg, N_PGM), lambda b, i: (b, i, 0)),
                  pl.BlockSpec((1, D_GM), lambda b, i: (0, 0)),
                  pl.BlockSpec((1, D_GM), lambda b, i: (0, 0)),
                  pl.BlockSpec((N_GM_HEADS * CHUNK, CHUNK), lambda b, i: (0, 0)),
                  pl.BlockSpec((CHUNK, D_GM), lambda b, i: (0, 0))],
        out_specs=pl.BlockSpec((1, tg, D_GM), lambda b, i: (b, i, 0)),
        compiler_params=_cparams("arbitrary", "arbitrary"),
        name="spatial_gate",
    )(pgm, ln_g, ln_b, ws, bias_tile)


BIAS_LANE = HEAD_DIM


def _foxprep_kernel(q_ref, k_ref, v_ref, fl_ref, bf_ref, qg_ref, kg_ref,
                    qa_ref, ka_ref, va_ref, carry_ref):
    tp = q_ref.shape[1]

    @pl.when(pl.program_id(1) == 0)
    def _():
        carry_ref[...] = jnp.zeros_like(carry_ref)

    z = fl_ref[0] + bf_ref[...]
    logf = jnp.minimum(z, 0.0) - jnp.log1p(jnp.exp(-jnp.abs(z)))
    tri = (lax.broadcasted_iota(jnp.int32, (CHUNK, CHUNK), 1) <=
           lax.broadcasted_iota(jnp.int32, (CHUNK, CHUNK), 0)).astype(F32)
    carry = carry_ref[0:1, :]
    parts = []
    for c in range(tp // CHUNK):
        cs = jnp.dot(tri, logf[c * CHUNK:(c + 1) * CHUNK], precision=lax.Precision.HIGHEST,
                     preferred_element_type=F32) + carry
        carry = cs[CHUNK - 1:CHUNK, :]
        parts.append(cs)
    carry_ref[...] = jnp.broadcast_to(carry, carry_ref.shape)
    neg_cum = -jnp.concatenate(parts, axis=0)
    hi = neg_cum.astype(BF16).astype(F32)
    r1 = neg_cum - hi
    mid = r1.astype(BF16).astype(F32)
    lo = (r1 - mid).astype(BF16).astype(F32)

    lane = _lane_iota((tp, LANES))
    low = lane < HEAD_DIM
    for p in range(N_FOX_HEADS // 2):
        cs_ = slice(p * LANES, (p + 1) * LANES)
        q = q_ref[0, :, cs_].astype(F32)
        k = k_ref[0, :, cs_].astype(F32)
        v = v_ref[0, :, cs_].astype(F32)
        qn = q * _pair_rms_scale(q) * (qg_ref[:, cs_] * (HEAD_DIM ** -0.5))
        kn = k * _pair_rms_scale(k) * kg_ref[:, cs_]
        for half in range(2):
            h = 2 * p + half
            if half == 0:
                qh, kh, vh = qn, kn, v
            else:
                qh, kh, vh = (pltpu.roll(t, HEAD_DIM, 1) for t in (qn, kn, v))
            is_bias = (lane >= BIAS_LANE) & (lane < BIAS_LANE + 3)
            qa = jnp.where(low, qh, jnp.where(is_bias, 1.0, 0.0))
            bias = jnp.where(lane == BIAS_LANE, hi[:, h:h + 1],
                             jnp.where(lane == BIAS_LANE + 1, mid[:, h:h + 1],
                                       jnp.where(lane == BIAS_LANE + 2, lo[:, h:h + 1], 0.0)))
            ka = jnp.where(low, kh, bias)
            va = jnp.where(low, vh, jnp.where(lane == HEAD_DIM, 1.0, 0.0))
            qa_ref[0, h] = qa.astype(BF16)
            ka_ref[0, h] = ka.astype(BF16)
            va_ref[0, h] = va.astype(BF16)


def _fox_prep(p384, fl, bf_row, qg, kg, tp):
    B, S, _ = p384.shape
    col = lambda j: pl.BlockSpec((1, tp, D_FOX), lambda b, i, j=j: (b, i, j))
    aug = jax.ShapeDtypeStruct((B, N_FOX_HEADS, S, LANES), BF16)
    aug_spec = pl.BlockSpec((1, N_FOX_HEADS, tp, LANES), lambda b, i: (b, 0, i, 0))
    return pl.pallas_call(
        _foxprep_kernel,
        out_shape=(aug, aug, aug),
        grid=(B, S // tp),
        in_specs=[col(4), col(5), col(6),
                  pl.BlockSpec((1, tp, LANES), lambda b, i: (b, i, 0)),
                  pl.BlockSpec((1, LANES), lambda b, i: (0, 0)),
                  pl.BlockSpec((1, D_FOX), lambda b, i: (0, 0)),
                  pl.BlockSpec((1, D_FOX), lambda b, i: (0, 0))],
        out_specs=(aug_spec, aug_spec, aug_spec),
        scratch_shapes=[pltpu.VMEM((SUBLANES, LANES), F32)],
        compiler_params=_cparams("arbitrary", "arbitrary"),
        name="fox_prep",
    )(p384, p384, p384, fl, bf_row, qg, kg)


def _fox_kernel(q_ref, k_ref, v_ref, o_ref, m_ref, acc_ref):
    tq = q_ref.shape[2]
    i = pl.program_id(2)
    dn_t = (((1,), (1,)), ((), ()))
    outs = []
    for hh in range(2):
        q = q_ref[0, hh]
        m_ref[...] = jnp.full(m_ref.shape, NEG_INF, F32)
        acc_ref[...] = jnp.zeros_like(acc_ref)

        def step(j, masked, q=q, hh=hh):
            start = pl.multiple_of(j * tq, tq)
            k = k_ref[0, hh, pl.ds(start, tq), :]
            v = v_ref[0, hh, pl.ds(start, tq), :]
            s = lax.dot_general(q, k, dn_t, preferred_element_type=F32)
            if masked:
                rr = lax.broadcasted_iota(jnp.int32, s.shape, 0)
                cc = lax.broadcasted_iota(jnp.int32, s.shape, 1)
                s = jnp.where(cc <= rr, s, NEG_INF)
            m_old = m_ref[...]
            m_new = jnp.maximum(m_old, jnp.max(s, axis=-1, keepdims=True))
            p = jnp.exp(s - m_new)
            acc_ref[...] = jnp.exp(m_old - m_new) * acc_ref[...] + jnp.dot(
                p.astype(BF16), v, preferred_element_type=F32)
            m_ref[...] = m_new

        def body(j, carry):
            step(j, False)
            return carry

        lax.fori_loop(0, i, body, 0)
        step(i, True)
        acc = acc_ref[...]
        outs.append(acc * (1.0 / acc[:, HEAD_DIM:HEAD_DIM + 1]))
    low = _lane_iota((tq, LANES)) < HEAD_DIM
    o_ref[0] = jnp.where(low, outs[0], pltpu.roll(outs[1], HEAD_DIM, 1)).astype(BF16)


def _fox_attention(qa, ka, va, tq):
    B, H, S, _ = qa.shape
    return pl.pallas_call(
        _fox_kernel,
        out_shape=jax.ShapeDtypeStruct((B, S, D_FOX), BF16),
        grid=(B, H // 2, S // tq),
        in_specs=[pl.BlockSpec((1, 2, tq, LANES), lambda b, p, i: (b, p, i, 0)),
                  pl.BlockSpec((1, 2, S, LANES), lambda b, p, i: (b, p, 0, 0)),
                  pl.BlockSpec((1, 2, S, LANES), lambda b, p, i: (b, p, 0, 0))],
        out_specs=pl.BlockSpec((1, tq, LANES), lambda b, p, i: (b, i, p)),
        scratch_shapes=[pltpu.VMEM((tq, 1), F32), pltpu.VMEM((tq, LANES), F32)],
        compiler_params=_cparams("arbitrary", "arbitrary", "arbitrary"),
        name="fox_attention",
    )(qa, ka, va)


def _outproj_kernel(yr_ref, yg_ref, yf_ref, x_ref, gt_ref, sc_ref, sh_ref, g_ref, w_ref,
                    xo_ref, h_ref):
    mix = jnp.concatenate([yr_ref[0], yg_ref[0], yf_ref[0]], axis=-1)
    upd = jnp.dot(mix, w_ref[...], preferred_element_type=F32)
    xn = x_ref[0] + gt_ref[0] * upd
    xo_ref[0] = xn
    h_ref[0] = _mod_rms(xn, g_ref[...], sc_ref[0], sh_ref[0]).astype(BF16)


def _out_projection(yr, yg, yf, x, gt, sc, sh, g, w_o, tm):
    B, S, D = x.shape
    row = lambda n: pl.BlockSpec((1, tm, n), lambda b, i: (b, i, 0))
    vec = pl.BlockSpec((1, 1, D), lambda b, i: (b, 0, 0))
    return pl.pallas_call(
        _outproj_kernel,
        out_shape=(jax.ShapeDtypeStruct((B, S, D), F32), jax.ShapeDtypeStruct((B, S, D), BF16)),
        grid=(B, S // tm),
        in_specs=[row(D_RET), row(D_GM), row(D_FOX), row(D), vec, vec, vec,
                  pl.BlockSpec((1, D), lambda b, i: (0, 0)),
                  _resident(w_o.shape)],
        out_specs=(row(D), row(D)),
        compiler_params=_cparams("arbitrary", "arbitrary"),
        name="out_proj",
    )(yr, yg, yf, x, gt, sc, sh, g, w_o)


FF_CHUNK = MXU_DIM


def _ffn_kernel(h_ref, x_ref, gt_ref, wu_ref, cw_ref, cb_ref, wd_ref, o_ref,
                carry_ref, stage_ref, act_ref):
    tm = h_ref.shape[1]

    @pl.when(pl.program_id(1) == 0)
    def _():
        carry_ref[...] = jnp.zeros_like(carry_ref)

    h = h_ref[0]

    def conv_cols(c0):
        cs = slice(c0, c0 + FF_CHUNK)
        up = jnp.dot(h, wu_ref[:, cs], preferred_element_type=F32)
        stage_ref[0:SUBLANES, :] = carry_ref[:, cs]
        stage_ref[SUBLANES:, :] = up
        carry_ref[:, cs] = up[tm - SUBLANES:, :]
        prev1 = stage_ref[SUBLANES - 1:SUBLANES - 1 + tm, :]
        prev2 = stage_ref[SUBLANES - 2:SUBLANES - 2 + tm, :]
        return (cw_ref[2:3, cs] * up + cw_ref[1:2, cs] * prev1 + cw_ref[0:1, cs] * prev2
                + cb_ref[:, cs])

    for f in range(D_FF // FF_CHUNK):
        a = conv_cols(f * FF_CHUNK)
        b = conv_cols(D_FF + f * FF_CHUNK)
        act_ref[:, f * FF_CHUNK:(f + 1) * FF_CHUNK] = ((a * jax.nn.sigmoid(a)) * b).astype(BF16)
    down = jnp.dot(act_ref[...], wd_ref[...], preferred_element_type=F32)
    o_ref[0] = x_ref[0] + gt_ref[0] * down


def _ffn(h2, x, gt, w_up, conv_w, conv_b, w_down, tm):
    B, S, D = x.shape
    return pl.pallas_call(
        _ffn_kernel,
        out_shape=jax.ShapeDtypeStruct((B, S, D), F32),
        grid=(B, S // tm),
        in_specs=[pl.BlockSpec((1, tm, D), lambda b, i: (b, i, 0)),
                  pl.BlockSpec((1, tm, D), lambda b, i: (b, i, 0)),
                  pl.BlockSpec((1, 1, D), lambda b, i: (b, 0, 0)),
                  _resident(w_up.shape), _resident(conv_w.shape), _resident(conv_b.shape),
                  _resident(w_down.shape)],
        out_specs=pl.BlockSpec((1, tm, D), lambda b, i: (b, i, 0)),
        scratch_shapes=[pltpu.VMEM((SUBLANES, 2 * D_FF), F32),
                        pltpu.VMEM((tm + SUBLANES, FF_CHUNK), F32),
                        pltpu.VMEM((tm, D_FF), BF16)],
        compiler_params=_cparams("arbitrary", "arbitrary"),
        name="conv_ffn",
    )(h2, x, gt, w_up, conv_w, conv_b, w_down)


def _permute_w_in(w):
    r0, g0, f0, l0 = 0, 4 * D_RET, 4 * D_RET + 2 * D_GM, 4 * D_RET + 2 * D_GM + 3 * D_FOX
    pad = jnp.zeros((w.shape[0], LANES - N_FOX_HEADS), w.dtype)
    return jnp.concatenate([w[:, r0:g0], w[:, f0:l0], w[:, g0:f0], w[:, l0:], pad],
                           axis=1).astype(BF16)


def kernel(x, c, ada_w, ada_b, norm1_g, w_in, ret_norm_g, gm_ln_g, gm_ln_b, gm_ws, gm_bs,
           fox_qn_g, fox_kn_g, fox_bf, w_o, norm2_g, w_up, conv_w, conv_b, w_down):
    B, S, D = x.shape
    L = ada_w.shape[0]
    assert D == D_MODEL and S % CHUNK == 0
    tm = min(S, 512)

    mod = _modulation(c, ada_w, ada_b).reshape(L, B, 6, 1, D)
    cos_t, sin_t = _rope_tables(S)

    for l in range(L):
        sh1, sc1, gt1, sh2, sc2, gt2 = (mod[l, :, i] for i in range(6))
        p384, pgm, fl = _in_projection(x, sc1, sh1, norm1_g[l].reshape(1, D),
                                       _permute_w_in(w_in[l]), tm)

        y_r = _retention(p384, cos_t, sin_t, ret_norm_g[l].reshape(1, D_RET), tm)

        bias_tile = jnp.repeat(gm_bs[l].T, HEAD_DIM, axis=1)
        y_g = _spatial_gate(pgm, gm_ln_g[l].reshape(1, D_GM), gm_ln_b[l].reshape(1, D_GM),
                            gm_ws[l].reshape(N_GM_HEADS * CHUNK, CHUNK), bias_tile, tm)

        bf_row = jnp.pad(fox_bf[l], (0, LANES - N_FOX_HEADS)).reshape(1, LANES)
        qa, ka, va = _fox_prep(p384, fl, bf_row,
                               jnp.tile(fox_qn_g[l], N_FOX_HEADS).reshape(1, D_FOX),
                               jnp.tile(fox_kn_g[l], N_FOX_HEADS).reshape(1, D_FOX), tm)
        y_f = _fox_attention(qa, ka, va, tm)

        x, h2 = _out_projection(y_r, y_g, y_f, x, gt1, sc2, sh2, norm2_g[l].reshape(1, D),
                                w_o[l].astype(BF16), tm)
        x = _ffn(h2, x, gt2, w_up[l].astype(BF16), conv_w[l], conv_b[l].reshape(1, 2 * D_FF),
                 w_down[l].astype(BF16), tm)
    return x
```

```python
import functools
import math

import jax
import jax.numpy as jnp
from jax import lax
from jax.experimental import pallas as pl
from jax.experimental.pallas import tpu as pltpu

D_MODEL = 1024
HEAD_DIM = 64
N_RET_HEADS = 6
N_GM_HEADS = 4
N_FOX_HEADS = 6
D_RET = N_RET_HEADS * HEAD_DIM
D_GM = N_GM_HEADS * HEAD_DIM
D_FOX = N_FOX_HEADS * HEAD_DIM
CHUNK = 128
D_FF = 2816
CONV_W = 3
ROPE_BASE = 10000.0
EPS = 1e-6
NEG_INF = -1e30

LANES = 128
SUBLANES = 8
MXU_DIM = 256
VMEM_LIMIT = 52 * 1024 * 1024

N_P384 = 7 * D_RET
N_PGM = 2 * D_GM
N_MAIN = 4 * D_RET + 2 * D_GM + 3 * D_FOX
GM_COL0 = 4 * D_RET
FOX_COL0 = GM_COL0 + 2 * D_GM

F32 = jnp.float32
BF16 = jnp.bfloat16


def _cparams(*sem):
    return pltpu.CompilerParams(dimension_semantics=sem, vmem_limit_bytes=VMEM_LIMIT)


def _resident(shape):
    nd = len(shape)
    return pl.BlockSpec(shape, lambda *_: (0,) * nd, pipeline_mode=pl.Buffered(1))


def _lane_iota(shape):
    return lax.broadcasted_iota(jnp.int32, shape, len(shape) - 1)


def _mod_kernel(c_ref, w_ref, b_ref, o_ref):
    c = c_ref[...]
    cond = c * jax.nn.sigmoid(c)
    o_ref[0] = jnp.dot(cond, w_ref[0], precision=lax.Precision.HIGHEST,
                       preferred_element_type=F32) + b_ref[0]


def _modulation(c, ada_w, ada_b):
    L, D, N = ada_w.shape
    B = c.shape[0]
    tn = 1536
    return pl.pallas_call(
        _mod_kernel,
        out_shape=jax.ShapeDtypeStruct((L, B, N), F32),
        grid=(L, N // tn),
        in_specs=[pl.BlockSpec((B, D), lambda l, j: (0, 0)),
                  pl.BlockSpec((1, D, tn), lambda l, j: (l, 0, j)),
                  pl.BlockSpec((1, 1, tn), lambda l, j: (l, 0, j))],
        out_specs=pl.BlockSpec((1, B, tn), lambda l, j: (l, 0, j)),
        compiler_params=_cparams("arbitrary", "arbitrary"),
        name="adaln_mod",
    )(c, ada_w, ada_b.reshape(L, 1, N))


def _rope_kernel(inv_ref, cos_ref, sin_ref):
    ts = cos_ref.shape[0]
    pos = (lax.broadcasted_iota(jnp.int32, (ts, LANES), 0) + pl.program_id(0) * ts).astype(F32)
    ang = pos * inv_ref[...]
    first_half = (_lane_iota((ts, LANES)) % HEAD_DIM) < (HEAD_DIM // 2)
    cos_ref[...] = jnp.cos(ang)
    s = jnp.sin(ang)
    sin_ref[...] = jnp.where(first_half, -s, s)


def _rope_tables(S):
    half = HEAD_DIM // 2
    inv = ROPE_BASE ** (-jnp.arange(half, dtype=F32) / half)
    inv_row = jnp.tile(inv, LANES // half).reshape(1, LANES)
    ts = min(S, 512)
    return pl.pallas_call(
        _rope_kernel,
        out_shape=(jax.ShapeDtypeStruct((S, LANES), F32),) * 2,
        grid=(S // ts,),
        in_specs=[pl.BlockSpec((1, LANES), lambda i: (0, 0))],
        out_specs=(pl.BlockSpec((ts, LANES), lambda i: (i, 0)),) * 2,
        compiler_params=_cparams("arbitrary"),
        name="rope_tables",
    )(inv_row)


def _mod_rms(x, g, sc, sh):
    ms = jnp.mean(x * x, axis=-1, keepdims=True)
    return (x * lax.rsqrt(ms + EPS) * g) * (1.0 + sc) + sh


def _inproj_kernel(x_ref, sc_ref, sh_ref, g_ref, w_ref, wf_ref, p384_ref, pgm_ref, fl_ref):
    h = _mod_rms(x_ref[0], g_ref[...], sc_ref[0], sh_ref[0]).astype(BF16)
    for lo in range(0, N_MAIN, MXU_DIM):
        width = min(MXU_DIM, N_MAIN - lo)
        r = jnp.dot(h, w_ref[:, lo:lo + width], preferred_element_type=F32)
        for off in range(0, width, LANES):
            c0 = lo + off
            piece = r[:, off:off + LANES].astype(BF16)
            if c0 < GM_COL0:
                p384_ref[0, :, c0:c0 + LANES] = piece
            elif c0 < FOX_COL0:
                pgm_ref[0, :, c0 - GM_COL0:c0 - GM_COL0 + LANES] = piece
            else:
                d0 = c0 - FOX_COL0 + 4 * D_RET
                p384_ref[0, :, d0:d0 + LANES] = piece
    fl_ref[0] = jnp.dot(h, wf_ref[...], preferred_element_type=F32)


def _in_projection(x, sc, sh, g, w_main, w_f, tm):
    B, S, D = x.shape
    return pl.pallas_call(
        _inproj_kernel,
        out_shape=(jax.ShapeDtypeStruct((B, S, N_P384), BF16),
                   jax.ShapeDtypeStruct((B, S, N_PGM), BF16),
                   jax.ShapeDtypeStruct((B, S, LANES), F32)),
        grid=(B, S // tm),
        in_specs=[pl.BlockSpec((1, tm, D), lambda b, i: (b, i, 0)),
                  pl.BlockSpec((1, 1, D), lambda b, i: (b, 0, 0)),
                  pl.BlockSpec((1, 1, D), lambda b, i: (b, 0, 0)),
                  pl.BlockSpec((1, D), lambda b, i: (0, 0)),
                  _resident(w_main.shape), _resident(w_f.shape)],
        out_specs=(pl.BlockSpec((1, tm, N_P384), lambda b, i: (b, i, 0)),
                   pl.BlockSpec((1, tm, N_PGM), lambda b, i: (b, i, 0)),
                   pl.BlockSpec((1, tm, LANES), lambda b, i: (b, i, 0))),
        compiler_params=_cparams("arbitrary", "arbitrary"),
        name="in_proj",
    )(x, sc, sh, g, w_main, w_f)


def _pair_rms_scale(x):
    low = _lane_iota(x.shape) < HEAD_DIM
    sq = x * x
    ss0 = jnp.sum(jnp.where(low, sq, 0.0), axis=-1, keepdims=True)
    ss1 = jnp.sum(jnp.where(low, 0.0, sq), axis=-1, keepdims=True)
    return jnp.where(low, lax.rsqrt(ss0 * (1.0 / HEAD_DIM) + EPS),
                     lax.rsqrt(ss1 * (1.0 / HEAD_DIM) + EPS))


def _swap_half_heads(x):
    first_half = (_lane_iota(x.shape) % HEAD_DIM) < (HEAD_DIM // 2)
    return jnp.where(first_half, pltpu.roll(x, LANES - HEAD_DIM // 2, 1),
                     pltpu.roll(x, HEAD_DIM // 2, 1))


def _ret_log_gamma(h):
    return math.log(1.0 - 2.0 ** (-5.0 - h))


def _ret_kernel(q_ref, k_ref, v_ref, g_ref, cos_ref, sin_ref, ng_ref, o_ref,
                state_ref, decay_ref, qdec_ref, kdec_ref):
    tr = q_ref.shape[1]
    n_pairs = N_RET_HEADS // 2
    low = _lane_iota((CHUNK, LANES)) < HEAD_DIM

    @pl.when(pl.program_id(1) == 0)
    def _():
        state_ref[...] = jnp.zeros_like(state_ref)

    row = lax.broadcasted_iota(jnp.int32, (CHUNK, CHUNK), 0)
    col = lax.broadcasted_iota(jnp.int32, (CHUNK, CHUNK), 1)
    rel = (row - col).astype(F32)
    idx = lax.broadcasted_iota(jnp.int32, (CHUNK, LANES), 0).astype(F32)
    for h in range(N_RET_HEADS):
        lg = _ret_log_gamma(h)
        decay_ref[h] = jnp.where(rel >= 0, jnp.exp(lg * jnp.maximum(rel, 0.0)), 0.0)
    for p in range(n_pairs):
        lg0, lg1 = _ret_log_gamma(2 * p), _ret_log_gamma(2 * p + 1)
        lgl = jnp.where(low, lg0, lg1)
        qdec_ref[p] = jnp.exp(lgl * (idx + 1.0))
        kdec_ref[p] = jnp.exp(lgl * (CHUNK - 1.0 - idx))

    blockdiag = (lax.broadcasted_iota(jnp.int32, (LANES, LANES), 0) < HEAD_DIM) == \
                (lax.broadcasted_iota(jnp.int32, (LANES, LANES), 1) < HEAD_DIM)
    dn_t = (((1,), (1,)), ((), ()))
    dn_ta = (((0,), (0,)), ((), ()))

    for p in range(n_pairs):
        cs = slice(p * LANES, (p + 1) * LANES)
        lg0, lg1 = _ret_log_gamma(2 * p), _ret_log_gamma(2 * p + 1)
        cd = jnp.where(low, math.exp(lg0 * CHUNK), math.exp(lg1 * CHUNK))[0:1, :]
        for c in range(tr // CHUNK):
            rs = slice(c * CHUNK, (c + 1) * CHUNK)
            cos, sin = cos_ref[rs, :], sin_ref[rs, :]
            q = q_ref[0, rs, cs].astype(F32)
            k = k_ref[0, rs, cs].astype(F32)
            q = q * cos + _swap_half_heads(q) * sin
            k = (k * cos + _swap_half_heads(k) * sin) * (HEAD_DIM ** -0.5)
            v = v_ref[0, rs, cs]
            kb = k.astype(BF16)
            q0 = jnp.where(low, q, 0.0).astype(BF16)
            q1 = jnp.where(low, 0.0, q).astype(BF16)
            s0 = lax.dot_general(q0, kb, dn_t, preferred_element_type=F32) * decay_ref[2 * p]
            s1 = lax.dot_general(q1, kb, dn_t, preferred_element_type=F32) * decay_ref[2 * p + 1]
            y0 = jnp.dot(s0.astype(BF16), v, preferred_element_type=F32)
            y1 = jnp.dot(s1.astype(BF16), v, preferred_element_type=F32)
            state = state_ref[p]
            y_cross = jnp.dot((q * qdec_ref[p]).astype(BF16), state.astype(BF16),
                              preferred_element_type=F32)
            y = jnp.where(low, y0, y1) + y_cross
            kv = lax.dot_general((k * kdec_ref[p]).astype(BF16), v, dn_ta,
                                 preferred_element_type=F32)
            state_ref[p] = state * cd + jnp.where(blockdiag, kv, 0.0)
            yn = y * _pair_rms_scale(y) * ng_ref[:, cs]
            gate = g_ref[0, rs, cs].astype(F32)
            o_ref[0, rs, cs] = (yn * (gate * jax.nn.sigmoid(gate))).astype(BF16)


def _retention(p384, cos_t, sin_t, ng, tr):
    B, S, _ = p384.shape
    col = lambda j: pl.BlockSpec((1, tr, D_RET), lambda b, i, j=j: (b, i, j))
    return pl.pallas_call(
        _ret_kernel,
        out_shape=jax.ShapeDtypeStruct((B, S, D_RET), BF16),
        grid=(B, S // tr),
        in_specs=[col(0), col(1), col(2), col(3),
                  pl.BlockSpec((tr, LANES), lambda b, i: (i, 0)),
                  pl.BlockSpec((tr, LANES), lambda b, i: (i, 0)),
                  pl.BlockSpec((1, D_RET), lambda b, i: (0, 0))],
        out_specs=pl.BlockSpec((1, tr, D_RET), lambda b, i: (b, i, 0)),
        scratch_shapes=[pltpu.VMEM((N_RET_HEADS // 2, LANES, LANES), F32),
                        pltpu.VMEM((N_RET_HEADS, CHUNK, CHUNK), F32),
                        pltpu.VMEM((N_RET_HEADS // 2, CHUNK, LANES), F32),
                        pltpu.VMEM((N_RET_HEADS // 2, CHUNK, LANES), F32)],
        compiler_params=_cparams("arbitrary", "arbitrary"),
        name="retention",
    )(p384, p384, p384, p384, cos_t, sin_t, ng)


def _gelu(x):
    return 0.5 * x * (1.0 + jnp.tanh(math.sqrt(2.0 / math.pi) * (x + 0.044715 * (x * x * x))))


def _gmlp_kernel(uv_ref, lg_ref, lb_ref, ws_ref, bias_ref, o_ref):
    tg = uv_ref.shape[1]
    u = _gelu(uv_ref[0, :, :D_GM].astype(F32))
    v = _gelu(uv_ref[0, :, D_GM:].astype(F32))
    mu = jnp.mean(v, axis=-1, keepdims=True)
    vc = v - mu
    var = jnp.mean(vc * vc, axis=-1, keepdims=True)
    vn = (vc * lax.rsqrt(var + EPS) * lg_ref[...] + lb_ref[...]).astype(BF16)
    r_i = lax.broadcasted_iota(jnp.int32, (N_GM_HEADS * CHUNK, CHUNK), 0) % CHUNK
    c_i = lax.broadcasted_iota(jnp.int32, (N_GM_HEADS * CHUNK, CHUNK), 1)
    w = jnp.where(c_i <= r_i, ws_ref[...], 0.0).astype(BF16)
    group = _lane_iota((CHUNK, D_GM)) // HEAD_DIM
    for c in range(tg // CHUNK):
        rs = slice(c * CHUNK, (c + 1) * CHUNK)
        r = jnp.dot(w, vn[rs], preferred_element_type=F32)
        mixed = r[0:CHUNK]
        for g in range(1, N_GM_HEADS):
            mixed = jnp.where(group == g, r[g * CHUNK:(g + 1) * CHUNK], mixed)
        o_ref[0, rs, :] = (u[rs] * (mixed + bias_ref[...])).astype(BF16)


def _spatial_gate(pgm, ln_g, ln_b, ws, bias_tile, tg):
    B, S, _ = pgm.shape
    return pl.pallas_call(
        _gmlp_kernel,
        out_shape=jax.ShapeDtypeStruct((B, S, D_GM), BF16),
        grid=(B, S // tg),
        in_specs=[pl.BlockSpec((1, tg, N_PGM), lambda b, i: (b, i, 0)),
                  pl.BlockSpec((1, D_GM), lambda b, i: (0, 0)),
                  pl.BlockSpec((1, D_GM), lambda b, i: (0, 0)),
                  pl.BlockSpec((N_GM_HEADS * CHUNK, CHUNK), lambda b, i: (0, 0)),
                  pl.BlockSpec((CHUNK, D_GM), lambda b, i: (0, 0))],
        out_specs=pl.BlockSpec((1, tg, D_GM), lambda b, i: (b, i, 0)),
        compiler_params=_cparams("arbitrary", "arbitrary"),
        name="spatial_gate",
    )(pgm, ln_g, ln_b, ws, bias_tile)


BIAS_LANE = HEAD_DIM
LOG2E = math.log2(math.e)
FOX_TK = 512
FOX_TQ = MXU_DIM
VT_ROWS = 80


def _foxprep_kernel(q_ref, k_ref, v_ref, fl_ref, bf_ref, qg_ref, kg_ref,
                    qa_ref, ka_ref, va_ref, carry_ref):
    tp = q_ref.shape[1]

    @pl.when(pl.program_id(1) == 0)
    def _():
        carry_ref[...] = jnp.zeros_like(carry_ref)

    z = fl_ref[0] + bf_ref[...]
    logf = jnp.minimum(z, 0.0) - jnp.log1p(jnp.exp(-jnp.abs(z)))
    tri = (lax.broadcasted_iota(jnp.int32, (CHUNK, CHUNK), 1) <=
           lax.broadcasted_iota(jnp.int32, (CHUNK, CHUNK), 0)).astype(F32)
    carry = carry_ref[0:1, :]
    parts = []
    for c in range(tp // CHUNK):
        cs = jnp.dot(tri, logf[c * CHUNK:(c + 1) * CHUNK], precision=lax.Precision.HIGHEST,
                     preferred_element_type=F32) + carry
        carry = cs[CHUNK - 1:CHUNK, :]
        parts.append(cs)
    carry_ref[...] = jnp.broadcast_to(carry, carry_ref.shape)
    neg_cum = jnp.concatenate(parts, axis=0) * (-LOG2E)
    hi = neg_cum.astype(BF16).astype(F32)
    r1 = neg_cum - hi
    mid = r1.astype(BF16).astype(F32)
    lo = (r1 - mid).astype(BF16).astype(F32)

    lane = _lane_iota((tp, LANES))
    low = lane < HEAD_DIM
    for p in range(N_FOX_HEADS // 2):
        cs_ = slice(p * LANES, (p + 1) * LANES)
        q = q_ref[0, :, cs_].astype(F32)
        k = k_ref[0, :, cs_].astype(F32)
        v = v_ref[0, :, cs_].astype(F32)
        qn = q * _pair_rms_scale(q) * (qg_ref[:, cs_] * (HEAD_DIM ** -0.5 * LOG2E))
        kn = k * _pair_rms_scale(k) * kg_ref[:, cs_]
        for half in range(2):
            h = 2 * p + half
            if half == 0:
                qh, kh, vh = qn, kn, v
            else:
                qh, kh, vh = (pltpu.roll(t, HEAD_DIM, 1) for t in (qn, kn, v))
            is_bias = (lane >= BIAS_LANE) & (lane < BIAS_LANE + 3)
            qa = jnp.where(low, qh, jnp.where(is_bias, 1.0, 0.0))
            bias = jnp.where(lane == BIAS_LANE, hi[:, h:h + 1],
                             jnp.where(lane == BIAS_LANE + 1, mid[:, h:h + 1],
                                       jnp.where(lane == BIAS_LANE + 2, lo[:, h:h + 1], 0.0)))
            ka = jnp.where(low, kh, bias)
            va = jnp.where(low, vh, jnp.where(lane == HEAD_DIM, 1.0, 0.0))
            qa_ref[0, h] = qa.astype(BF16)
            ka_ref[0, h] = ka.astype(BF16)
            va_ref[0, h, 0] = va.T.astype(BF16)


def _fox_prep(p384, fl, bf_row, qg, kg):
    B, S, _ = p384.shape
    tp = FOX_TK
    col = lambda j: pl.BlockSpec((1, tp, D_FOX), lambda b, i, j=j: (b, i, j))
    aug = jax.ShapeDtypeStruct((B, N_FOX_HEADS, S, LANES), BF16)
    aug_spec = pl.BlockSpec((1, N_FOX_HEADS, tp, LANES), lambda b, i: (b, 0, i, 0))
    vt = jax.ShapeDtypeStruct((B, N_FOX_HEADS, S // tp, LANES, tp), BF16)
    vt_spec = pl.BlockSpec((1, N_FOX_HEADS, 1, LANES, tp), lambda b, i: (b, 0, i, 0, 0))
    return pl.pallas_call(
        _foxprep_kernel,
        out_shape=(aug, aug, vt),
        grid=(B, S // tp),
        in_specs=[col(4), col(5), col(6),
                  pl.BlockSpec((1, tp, LANES), lambda b, i: (b, i, 0)),
                  pl.BlockSpec((1, LANES), lambda b, i: (0, 0)),
                  pl.BlockSpec((1, D_FOX), lambda b, i: (0, 0)),
                  pl.BlockSpec((1, D_FOX), lambda b, i: (0, 0))],
        out_specs=(aug_spec, aug_spec, vt_spec),
        scratch_shapes=[pltpu.VMEM((SUBLANES, LANES), F32)],
        compiler_params=_cparams("arbitrary", "arbitrary"),
        name="fox_prep",
    )(p384, p384, p384, fl, bf_row, qg, kg)


def _fox_kernel(q_ref, k_ref, vt_ref, o_ref, m_ref, acc_ref, s0_ref, s1_ref):
    s_refs = (s0_ref, s1_ref)
    i = pl.program_id(2)
    n_qt = FOX_TK // FOX_TQ
    units = [(hh, qt) for hh in range(2) for qt in range(n_qt)]
    n_u = len(units)
    dn_t = (((1,), (1,)), ((), ()))
    m_ref[...] = jnp.full(m_ref.shape, NEG_INF, F32)
    acc_ref[...] = jnp.zeros_like(acc_ref)

    def scores(j, slot, u):
        hh, qt = units[u]
        start = pl.multiple_of(j * FOX_TK, FOX_TK)
        k = k_ref[0, hh, pl.ds(start, FOX_TK), :]
        q = q_ref[0, hh, qt * FOX_TQ:(qt + 1) * FOX_TQ, :]
        s_refs[slot][u] = lax.dot_general(k, q, dn_t, preferred_element_type=F32)

    def update(j, slot, u, diagonal):
        hh, qt = units[u]
        nk = (qt + 1) * FOX_TQ if diagonal else FOX_TK
        s = s_refs[slot][u, 0:nk, :]
        if diagonal:
            key = lax.broadcasted_iota(jnp.int32, s.shape, 0)
            qry = lax.broadcasted_iota(jnp.int32, s.shape, 1)
            s = jnp.where(key <= qry + qt * FOX_TQ, s, NEG_INF)
        m_old = m_ref[hh, qt]
        m_new = jnp.maximum(m_old, jnp.max(s, axis=0, keepdims=True))
        p = jnp.exp2((s - m_new).astype(BF16))
        vt = vt_ref[0, hh, j, 0:VT_ROWS, 0:nk]
        acc_ref[hh, qt] = jnp.exp2(m_old - m_new) * acc_ref[hh, qt] + jnp.dot(
            vt, p, preferred_element_type=F32)
        m_ref[hh, qt] = m_new

    for u in range(n_u):
        scores(0, 0, u)

    def body(j, carry):
        for u in range(n_u):
            scores(2 * j + 1, 1, u)
            update(2 * j, 0, u, False)
        for u in range(n_u):
            scores(2 * j + 2, 0, u)
            update(2 * j + 1, 1, u, False)
        return carry

    lax.fori_loop(0, i // 2, body, 0)

    @pl.when(i % 2 == 0)
    def _():
        for u in range(n_u):
            update(i, 0, u, True)

    @pl.when(i % 2 == 1)
    def _():
        for u in range(n_u):
            scores(i, 1, u)
            update(i - 1, 0, u, False)
        for u in range(n_u):
            update(i, 1, u, True)

    for qt in range(n_qt):
        rows = []
        for hh in range(2):
            acc = acc_ref[hh, qt]
            rows.append(acc[0:HEAD_DIM] * (1.0 / acc[HEAD_DIM:HEAD_DIM + 1]))
        pair_t = jnp.concatenate(rows, axis=0)
        o_ref[0, qt * FOX_TQ:(qt + 1) * FOX_TQ, :] = pair_t.T.astype(BF16)


def _fox_attention(qa, ka, vt):
    B, H, S, _ = qa.shape
    n_qt = FOX_TK // FOX_TQ
    return pl.pallas_call(
        _fox_kernel,
        out_shape=jax.ShapeDtypeStruct((B, S, D_FOX), BF16),
        grid=(B, H // 2, S // FOX_TK),
        in_specs=[pl.BlockSpec((1, 2, FOX_TK, LANES), lambda b, p, i: (b, p, i, 0)),
                  pl.BlockSpec((1, 2, S, LANES), lambda b, p, i: (b, p, 0, 0)),
                  pl.BlockSpec((1, 2, S // FOX_TK, LANES, FOX_TK), lambda b, p, i: (b, p, 0, 0, 0))],
        out_specs=pl.BlockSpec((1, FOX_TK, LANES), lambda b, p, i: (b, i, p)),
        scratch_shapes=[pltpu.VMEM((2, n_qt, 1, FOX_TQ), F32),
                        pltpu.VMEM((2, n_qt, VT_ROWS, FOX_TQ), F32),
                        pltpu.VMEM((2 * n_qt, FOX_TK, FOX_TQ), F32),
                        pltpu.VMEM((2 * n_qt, FOX_TK, FOX_TQ), F32)],
        compiler_params=_cparams("arbitrary", "arbitrary", "arbitrary"),
        name="fox_attention",
    )(qa, ka, vt)


def _outproj_kernel(yr_ref, yg_ref, yf_ref, x_ref, gt_ref, sc_ref, sh_ref, g_ref, w_ref,
                    xo_ref, h_ref):
    mix = jnp.concatenate([yr_ref[0], yg_ref[0], yf_ref[0]], axis=-1)
    upd = jnp.dot(mix, w_ref[...], preferred_element_type=F32)
    xn = x_ref[0] + gt_ref[0] * upd
    xo_ref[0] = xn
    h_ref[0] = _mod_rms(xn, g_ref[...], sc_ref[0], sh_ref[0]).astype(BF16)


def _out_projection(yr, yg, yf, x, gt, sc, sh, g, w_o, tm):
    B, S, D = x.shape
    row = lambda n: pl.BlockSpec((1, tm, n), lambda b, i: (b, i, 0))
    vec = pl.BlockSpec((1, 1, D), lambda b, i: (b, 0, 0))
    return pl.pallas_call(
        _outproj_kernel,
        out_shape=(jax.ShapeDtypeStruct((B, S, D), F32), jax.ShapeDtypeStruct((B, S, D), BF16)),
        grid=(B, S // tm),
        in_specs=[row(D_RET), row(D_GM), row(D_FOX), row(D), vec, vec, vec,
                  pl.BlockSpec((1, D), lambda b, i: (0, 0)),
                  _resident(w_o.shape)],
        out_specs=(row(D), row(D)),
        compiler_params=_cparams("arbitrary", "arbitrary"),
        name="out_proj",
    )(yr, yg, yf, x, gt, sc, sh, g, w_o)


FF_CHUNK = MXU_DIM


def _ffn_kernel(h_ref, x_ref, gt_ref, wu_ref, cw_ref, cb_ref, wd_ref, o_ref,
                carry_ref, stage_ref, act_ref):
    tm = h_ref.shape[1]

    @pl.when(pl.program_id(1) == 0)
    def _():
        carry_ref[...] = jnp.zeros_like(carry_ref)

    h = h_ref[0]

    def conv_cols(c0):
        cs = slice(c0, c0 + FF_CHUNK)
        up = jnp.dot(h, wu_ref[:, cs], preferred_element_type=F32)
        stage_ref[0:SUBLANES, :] = carry_ref[:, cs]
        stage_ref[SUBLANES:, :] = up
        carry_ref[:, cs] = up[tm - SUBLANES:, :]
        prev1 = stage_ref[SUBLANES - 1:SUBLANES - 1 + tm, :]
        prev2 = stage_ref[SUBLANES - 2:SUBLANES - 2 + tm, :]
        return (cw_ref[2:3, cs] * up + cw_ref[1:2, cs] * prev1 + cw_ref[0:1, cs] * prev2
                + cb_ref[:, cs])

    for f in range(D_FF // FF_CHUNK):
        a = conv_cols(f * FF_CHUNK)
        b = conv_cols(D_FF + f * FF_CHUNK)
        act_ref[:, f * FF_CHUNK:(f + 1) * FF_CHUNK] = ((a * jax.nn.sigmoid(a)) * b).astype(BF16)
    down = jnp.dot(act_ref[...], wd_ref[...], preferred_element_type=F32)
    o_ref[0] = x_ref[0] + gt_ref[0] * down


def _ffn(h2, x, gt, w_up, conv_w, conv_b, w_down, tm):
    B, S, D = x.shape
    return pl.pallas_call(
        _ffn_kernel,
        out_shape=jax.ShapeDtypeStruct((B, S, D), F32),
        grid=(B, S // tm),
        in_specs=[pl.BlockSpec((1, tm, D), lambda b, i: (b, i, 0)),
                  pl.BlockSpec((1, tm, D), lambda b, i: (b, i, 0)),
                  pl.BlockSpec((1, 1, D), lambda b, i: (b, 0, 0)),
                  _resident(w_up.shape), _resident(conv_w.shape), _resident(conv_b.shape),
                  _resident(w_down.shape)],
        out_specs=pl.BlockSpec((1, tm, D), lambda b, i: (b, i, 0)),
        scratch_shapes=[pltpu.VMEM((SUBLANES, 2 * D_FF), F32),
                        pltpu.VMEM((tm + SUBLANES, FF_CHUNK), F32),
                        pltpu.VMEM((tm, D_FF), BF16)],
        compiler_params=_cparams("arbitrary", "arbitrary"),
        name="conv_ffn",
    )(h2, x, gt, w_up, conv_w, conv_b, w_down)


def kernel(x, c, ada_w, ada_b, norm1_g, w_in, ret_norm_g, gm_ln_g, gm_ln_b, gm_ws, gm_bs,
           fox_qn_g, fox_kn_g, fox_bf, w_o, norm2_g, w_up, conv_w, conv_b, w_down):
    B, S, D = x.shape
    L = ada_w.shape[0]
    assert D == D_MODEL and S % FOX_TK == 0
    tm = 512

    mod = _modulation(c, ada_w, ada_b).reshape(L, B, 6, 1, D)
    cos_t, sin_t = _rope_tables(S)

    for l in range(L):
        sh1, sc1, gt1, sh2, sc2, gt2 = (mod[l, :, i] for i in range(6))
        w_main = w_in[l, :, :N_MAIN].astype(BF16)
        w_f = jnp.pad(w_in[l, :, N_MAIN:], ((0, 0), (0, LANES - N_FOX_HEADS))).astype(BF16)
        p384, pgm, fl = _in_projection(x, sc1, sh1, norm1_g[l].reshape(1, D), w_main, w_f, tm)

        y_r = _retention(p384, cos_t, sin_t, ret_norm_g[l].reshape(1, D_RET), tm)

        bias_tile = jnp.repeat(gm_bs[l].T, HEAD_DIM, axis=1)
        y_g = _spatial_gate(pgm, gm_ln_g[l].reshape(1, D_GM), gm_ln_b[l].reshape(1, D_GM),
                            gm_ws[l].reshape(N_GM_HEADS * CHUNK, CHUNK), bias_tile, tm)

        bf_row = jnp.pad(fox_bf[l], (0, LANES - N_FOX_HEADS)).reshape(1, LANES)
        qa, ka, vt = _fox_prep(p384, fl, bf_row,
                               jnp.tile(fox_qn_g[l], N_FOX_HEADS).reshape(1, D_FOX),
                               jnp.tile(fox_kn_g[l], N_FOX_HEADS).reshape(1, D_FOX))
        y_f = _fox_attention(qa, ka, vt)

        x, h2 = _out_projection(y_r, y_g, y_f, x, gt1, sc2, sh2, norm2_g[l].reshape(1, D),
                                w_o[l].astype(BF16), tm)
        x = _ffn(h2, x, gt2, w_up[l].astype(BF16), conv_w[l], conv_b[l].reshape(1, 2 * D_FF),
                 w_down[l].astype(BF16), tm)
    return x
```

```python
import math

import jax
import jax.numpy as jnp
from jax import lax
from jax.experimental import pallas as pl
from jax.experimental.pallas import tpu as pltpu

D_MODEL = 1024
HEAD_DIM = 64
N_RET_HEADS = 6
N_GM_HEADS = 4
N_FOX_HEADS = 6
D_RET = N_RET_HEADS * HEAD_DIM
D_GM = N_GM_HEADS * HEAD_DIM
D_FOX = N_FOX_HEADS * HEAD_DIM
CHUNK = 128
D_FF = 2816
ROPE_BASE = 10000.0
EPS = 1e-6
NEG_INF = -1e30

LANES = 128
SUBLANES = 8
MXU_DIM = 256
VMEM_LIMIT = 52 * 1024 * 1024

N_P384 = 7 * D_RET
N_PGM = 2 * D_GM
N_MAIN = 4 * D_RET + 2 * D_GM + 3 * D_FOX
GM_COL0 = 4 * D_RET
FOX_COL0 = GM_COL0 + 2 * D_GM

F32 = jnp.float32
BF16 = jnp.bfloat16


def _cparams(*sem):
    return pltpu.CompilerParams(dimension_semantics=sem, vmem_limit_bytes=VMEM_LIMIT)


def _resident(shape):
    nd = len(shape)
    return pl.BlockSpec(shape, lambda *_: (0,) * nd, pipeline_mode=pl.Buffered(1))


def _lane_iota(shape):
    return lax.broadcasted_iota(jnp.int32, shape, len(shape) - 1)


def _mod_kernel(c_ref, w_ref, b_ref, o_ref):
    c = c_ref[...]
    cond = c * jax.nn.sigmoid(c)
    o_ref[0] = jnp.dot(cond, w_ref[0], precision=lax.Precision.HIGHEST,
                       preferred_element_type=F32) + b_ref[0]


def _modulation(c, ada_w, ada_b):
    L, D, N = ada_w.shape
    B = c.shape[0]
    tn = 1536
    return pl.pallas_call(
        _mod_kernel,
        out_shape=jax.ShapeDtypeStruct((L, B, N), F32),
        grid=(L, N // tn),
        in_specs=[pl.BlockSpec((B, D), lambda l, j: (0, 0)),
                  pl.BlockSpec((1, D, tn), lambda l, j: (l, 0, j)),
                  pl.BlockSpec((1, 1, tn), lambda l, j: (l, 0, j))],
        out_specs=pl.BlockSpec((1, B, tn), lambda l, j: (l, 0, j)),
        compiler_params=_cparams("arbitrary", "arbitrary"),
        name="adaln_mod",
    )(c, ada_w, ada_b.reshape(L, 1, N))


def _ret_log_gamma(h):
    return math.log(1.0 - 2.0 ** (-5.0 - h))


def _rope_kernel(inv_ref, cos_ref, sin_ref, dq_ref, dk_ref):
    ts = cos_ref.shape[0]
    pos = (lax.broadcasted_iota(jnp.int32, (ts, LANES), 0) + pl.program_id(0) * ts).astype(F32)
    ang = pos * inv_ref[...]
    first_half = (_lane_iota((ts, LANES)) % HEAD_DIM) < (HEAD_DIM // 2)
    cos_ref[...] = jnp.cos(ang)
    s = jnp.sin(ang)
    sin_ref[...] = jnp.where(first_half, -s, s)

    head = _lane_iota((CHUNK, D_RET)) // HEAD_DIM
    lg = jnp.zeros((CHUNK, D_RET), F32)
    for h in range(N_RET_HEADS):
        lg = jnp.where(head == h, _ret_log_gamma(h), lg)
    t1 = lax.broadcasted_iota(jnp.int32, (CHUNK, D_RET), 0).astype(F32) + 1.0
    dq_ref[...] = jnp.exp(lg * t1)
    dk_ref[...] = jnp.exp(-lg * t1) * (HEAD_DIM ** -0.5)


def _rope_tables(S):
    half = HEAD_DIM // 2
    inv = ROPE_BASE ** (-jnp.arange(half, dtype=F32) / half)
    inv_row = jnp.tile(inv, LANES // half).reshape(1, LANES)
    ts = 512
    decay = jax.ShapeDtypeStruct((CHUNK, D_RET), F32)
    decay_spec = pl.BlockSpec((CHUNK, D_RET), lambda i: (0, 0))
    return pl.pallas_call(
        _rope_kernel,
        out_shape=(jax.ShapeDtypeStruct((S, LANES), F32),) * 2 + (decay, decay),
        grid=(S // ts,),
        in_specs=[pl.BlockSpec((1, LANES), lambda i: (0, 0))],
        out_specs=(pl.BlockSpec((ts, LANES), lambda i: (i, 0)),) * 2 + (decay_spec, decay_spec),
        compiler_params=_cparams("arbitrary"),
        name="rope_tables",
    )(inv_row)


def _mod_rms(x, g, sc, sh):
    ms = jnp.mean(x * x, axis=-1, keepdims=True)
    return (x * lax.rsqrt(ms + EPS) * g) * (1.0 + sc) + sh


def _swap_half_heads(x):
    first_half = (_lane_iota(x.shape) % HEAD_DIM) < (HEAD_DIM // 2)
    return jnp.where(first_half, pltpu.roll(x, LANES - HEAD_DIM // 2, 1),
                     pltpu.roll(x, HEAD_DIM // 2, 1))


def _gelu(x):
    return 0.5 * x * (1.0 + jnp.tanh(math.sqrt(2.0 / math.pi) * (x + 0.044715 * (x * x * x))))


def _inproj_kernel(x_ref, sc_ref, sh_ref, g_ref, w_ref, wf_ref, cos_ref, sin_ref, dq_ref, dk_ref,
                   lng_ref, lnb_ref, p384_ref, pgm_ref, fl_ref):
    tm = x_ref.shape[1]
    h = _mod_rms(x_ref[0], g_ref[...], sc_ref[0], sh_ref[0]).astype(BF16)

    def rotary_decay(piece, table_ref, tile):
        rot = piece * cos_ref[...] + _swap_half_heads(piece) * sin_ref[...]
        table = table_ref[:, tile * LANES:(tile + 1) * LANES]
        return rot * jnp.concatenate([table] * (tm // CHUNK), axis=0)

    for lo in range(0, N_MAIN, MXU_DIM):
        width = min(MXU_DIM, N_MAIN - lo)
        r = jnp.dot(h, w_ref[0, :, lo:lo + width].astype(BF16), preferred_element_type=F32)
        if lo == GM_COL0 + D_GM:
            v = _gelu(r)
            mu = jnp.mean(v, axis=-1, keepdims=True)
            vc = v - mu
            var = jnp.mean(vc * vc, axis=-1, keepdims=True)
            r = vc * lax.rsqrt(var + EPS) * lng_ref[...] + lnb_ref[...]
        for off in range(0, width, LANES):
            c0 = lo + off
            piece = r[:, off:off + LANES]
            if c0 < D_RET:
                piece = rotary_decay(piece, dq_ref, c0 // LANES)
            elif c0 < 2 * D_RET:
                piece = rotary_decay(piece, dk_ref, (c0 - D_RET) // LANES)
            elif 3 * D_RET <= c0 < 4 * D_RET:
                piece = piece * jax.nn.sigmoid(piece)
            elif GM_COL0 <= c0 < GM_COL0 + D_GM:
                piece = _gelu(piece)
            piece = piece.astype(BF16)
            if c0 < GM_COL0:
                p384_ref[0, :, c0:c0 + LANES] = piece
            elif c0 < FOX_COL0:
                pgm_ref[0, :, c0 - GM_COL0:c0 - GM_COL0 + LANES] = piece
            else:
                d0 = c0 - FOX_COL0 + 4 * D_RET
                p384_ref[0, :, d0:d0 + LANES] = piece
    fl_ref[0] = jnp.dot(h, wf_ref[...], preferred_element_type=F32)


def _in_projection(x, sc, sh, g, w_in, layer, w_f, cos_t, sin_t, dq, dk, ln_g, ln_b, tm):
    B, S, D = x.shape
    n_in = w_in.shape[2]
    const = lambda shape: pl.BlockSpec(shape, lambda b, i: (0,) * len(shape))
    return pl.pallas_call(
        _inproj_kernel,
        out_shape=(jax.ShapeDtypeStruct((B, S, N_P384), BF16),
                   jax.ShapeDtypeStruct((B, S, N_PGM), BF16),
                   jax.ShapeDtypeStruct((B, S, LANES), F32)),
        grid=(B, S // tm),
        in_specs=[pl.BlockSpec((1, tm, D), lambda b, i: (b, i, 0)),
                  pl.BlockSpec((1, 1, D), lambda b, i: (b, 0, 0)),
                  pl.BlockSpec((1, 1, D), lambda b, i: (b, 0, 0)),
                  const((1, D)),
                  pl.BlockSpec((1, D, n_in), lambda b, i: (layer, 0, 0),
                               pipeline_mode=pl.Buffered(1)),
                  _resident(w_f.shape),
                  pl.BlockSpec((tm, LANES), lambda b, i: (i, 0)),
                  pl.BlockSpec((tm, LANES), lambda b, i: (i, 0)),
                  const((CHUNK, D_RET)), const((CHUNK, D_RET)),
                  const((1, D_GM)), const((1, D_GM))],
        out_specs=(pl.BlockSpec((1, tm, N_P384), lambda b, i: (b, i, 0)),
                   pl.BlockSpec((1, tm, N_PGM), lambda b, i: (b, i, 0)),
                   pl.BlockSpec((1, tm, LANES), lambda b, i: (b, i, 0))),
        compiler_params=_cparams("arbitrary", "arbitrary"),
        name="in_proj",
    )(x, sc, sh, g, w_in, w_f, cos_t, sin_t, dq, dk, ln_g, ln_b)


def _pair_rms_scale(x):
    low = _lane_iota(x.shape) < HEAD_DIM
    sq = x * x
    ss0 = jnp.sum(jnp.where(low, sq, 0.0), axis=-1, keepdims=True)
    ss1 = jnp.sum(jnp.where(low, 0.0, sq), axis=-1, keepdims=True)
    return jnp.where(low, lax.rsqrt(ss0 * (1.0 / HEAD_DIM) + EPS),
                     lax.rsqrt(ss1 * (1.0 / HEAD_DIM) + EPS))


def _ret_kernel(q_ref, k_ref, v_ref, g_ref, ng_ref, o_ref, state_ref):
    tr = q_ref.shape[1]
    n_pairs = N_RET_HEADS // 2
    low = _lane_iota((CHUNK, LANES)) < HEAD_DIM
    keep0 = jnp.where(low, 1.0, 0.0).astype(BF16)
    keep1 = jnp.where(low, 0.0, 1.0).astype(BF16)
    causal = (lax.broadcasted_iota(jnp.int32, (CHUNK, CHUNK), 0) >=
              lax.broadcasted_iota(jnp.int32, (CHUNK, CHUNK), 1))
    blockdiag = (lax.broadcasted_iota(jnp.int32, (LANES, LANES), 0) < HEAD_DIM) == low
    dn_t = (((1,), (1,)), ((), ()))
    dn_ta = (((0,), (0,)), ((), ()))

    @pl.when(pl.program_id(1) == 0)
    def _():
        state_ref[...] = jnp.zeros_like(state_ref)

    n_c = tr // CHUNK
    tiles = [(p, c) for p in range(n_pairs) for c in range(n_c)]
    sl = lambda p, c: (slice(c * CHUNK, (c + 1) * CHUNK), slice(p * LANES, (p + 1) * LANES))
    kvs, scores = {}, {}
    for p, c in tiles:
        rs, cs = sl(p, c)
        q, k, v = q_ref[0, rs, cs], k_ref[0, rs, cs], v_ref[0, rs, cs]
        kvs[p, c] = lax.dot_general(k, v, dn_ta, preferred_element_type=F32)
        for hh, keep in enumerate((keep0, keep1)):
            s = lax.dot_general(q * keep, k, dn_t, preferred_element_type=F32)
            scores[p, c, hh] = jnp.where(causal, s, 0.0).astype(BF16)
    states = {}
    for p in range(n_pairs):
        chunk_decay = jnp.where(low[0:1, :], math.exp(_ret_log_gamma(2 * p) * CHUNK),
                                math.exp(_ret_log_gamma(2 * p + 1) * CHUNK))
        state = state_ref[p]
        for c in range(n_c):
            states[p, c] = state.astype(BF16)
            state = (state + jnp.where(blockdiag, kvs[p, c], 0.0)) * chunk_decay
        state_ref[p] = state
    for p, c in tiles:
        rs, cs = sl(p, c)
        q = q_ref[0, rs, cs]
        rhs = jnp.concatenate([v_ref[0, rs, cs], states[p, c]], axis=0)
        ys = [jnp.dot(jnp.concatenate([scores[p, c, hh], q * keep], axis=1), rhs,
                      preferred_element_type=F32) for hh, keep in enumerate((keep0, keep1))]
        y = jnp.where(low, ys[0], ys[1])
        yn = y * _pair_rms_scale(y) * ng_ref[:, cs]
        o_ref[0, rs, cs] = (yn * g_ref[0, rs, cs].astype(F32)).astype(BF16)


def _retention(p384, ng, tr):
    B, S, _ = p384.shape
    col = lambda j: pl.BlockSpec((1, tr, D_RET), lambda b, i, j=j: (b, i, j))
    return pl.pallas_call(
        _ret_kernel,
        out_shape=jax.ShapeDtypeStruct((B, S, D_RET), BF16),
        grid=(B, S // tr),
        in_specs=[col(0), col(1), col(2), col(3),
                  pl.BlockSpec((1, D_RET), lambda b, i: (0, 0))],
        out_specs=pl.BlockSpec((1, tr, D_RET), lambda b, i: (b, i, 0)),
        scratch_shapes=[pltpu.VMEM((N_RET_HEADS // 2, LANES, LANES), F32)],
        compiler_params=_cparams("arbitrary", "arbitrary"),
        name="retention",
    )(p384, p384, p384, p384, ng)


def _gmlp_kernel(uv_ref, ws_ref, bias_ref, o_ref):
    tg = uv_ref.shape[1]
    r_i = lax.broadcasted_iota(jnp.int32, (N_GM_HEADS * CHUNK, CHUNK), 0) % CHUNK
    c_i = lax.broadcasted_iota(jnp.int32, (N_GM_HEADS * CHUNK, CHUNK), 1)
    w = jnp.where(c_i <= r_i, ws_ref[...], 0.0).astype(BF16)
    group = _lane_iota((CHUNK, D_GM)) // HEAD_DIM
    for c in range(tg // CHUNK):
        rs = slice(c * CHUNK, (c + 1) * CHUNK)
        r = jnp.dot(w, uv_ref[0, rs, D_GM:], preferred_element_type=F32)
        mixed = r[0:CHUNK]
        for g in range(1, N_GM_HEADS):
            mixed = jnp.where(group == g, r[g * CHUNK:(g + 1) * CHUNK], mixed)
        u = uv_ref[0, rs, :D_GM].astype(F32)
        o_ref[0, rs, :] = (u * (mixed + bias_ref[...])).astype(BF16)


def _spatial_gate(pgm, ws, bias_tile, tg):
    B, S, _ = pgm.shape
    return pl.pallas_call(
        _gmlp_kernel,
        out_shape=jax.ShapeDtypeStruct((B, S, D_GM), BF16),
        grid=(B, S // tg),
        in_specs=[pl.BlockSpec((1, tg, N_PGM), lambda b, i: (b, i, 0)),
                  pl.BlockSpec((N_GM_HEADS * CHUNK, CHUNK), lambda b, i: (0, 0)),
                  pl.BlockSpec((CHUNK, D_GM), lambda b, i: (0, 0))],
        out_specs=pl.BlockSpec((1, tg, D_GM), lambda b, i: (b, i, 0)),
        compiler_params=_cparams("arbitrary", "arbitrary"),
        name="spatial_gate",
    )(pgm, ws, bias_tile)


LOG2E = math.log2(math.e)
FOX_TK = 512
FOX_TQ = MXU_DIM
FOX_BQ = 2 * FOX_TK
VT_ROWS = 80


def _foxprep_kernel(q_ref, k_ref, v_ref, fl_ref, bf_ref, qg_ref, kg_ref,
                    qa_ref, ka_ref, vt_ref, carry_ref):
    tp = q_ref.shape[1]

    @pl.when(pl.program_id(1) == 0)
    def _():
        carry_ref[...] = jnp.zeros_like(carry_ref)

    z = fl_ref[0] + bf_ref[...]
    logf = jnp.minimum(z, 0.0) - jnp.log1p(jnp.exp(-jnp.abs(z)))
    tri = (lax.broadcasted_iota(jnp.int32, (CHUNK, CHUNK), 1) <=
           lax.broadcasted_iota(jnp.int32, (CHUNK, CHUNK), 0)).astype(F32)
    carry = carry_ref[0:1, :]
    parts = []
    for c in range(tp // CHUNK):
        cs = jnp.dot(tri, logf[c * CHUNK:(c + 1) * CHUNK], precision=lax.Precision.HIGHEST,
                     preferred_element_type=F32) + carry
        carry = cs[CHUNK - 1:CHUNK, :]
        parts.append(cs)
    carry_ref[...] = jnp.broadcast_to(carry, carry_ref.shape)
    neg_cum = jnp.concatenate(parts, axis=0) * (-LOG2E)
    hi = neg_cum.astype(BF16).astype(F32)
    r1 = neg_cum - hi
    mid = r1.astype(BF16).astype(F32)
    lo = (r1 - mid).astype(BF16).astype(F32)

    lane = _lane_iota((tp, LANES))
    low = lane < HEAD_DIM
    ones_row = jnp.where(lax.broadcasted_iota(jnp.int32, (VT_ROWS - HEAD_DIM, tp), 0) == 0,
                         1.0, 0.0).astype(BF16)
    for p in range(N_FOX_HEADS // 2):
        cs_ = slice(p * LANES, (p + 1) * LANES)
        q = q_ref[0, :, cs_].astype(F32)
        k = k_ref[0, :, cs_].astype(F32)
        qn = q * _pair_rms_scale(q) * (qg_ref[:, cs_] * (HEAD_DIM ** -0.5 * LOG2E))
        kn = k * _pair_rms_scale(k) * kg_ref[:, cs_]
        v_t = v_ref[0, :, cs_].astype(F32).T.astype(BF16)
        for half in range(2):
            h = 2 * p + half
            own = low if half == 0 else jnp.logical_not(low)
            b0 = HEAD_DIM * (1 - half)
            ones = jnp.where((lane >= b0) & (lane < b0 + 3), 1.0, 0.0)
            bias = jnp.where(lane == b0, hi[:, h:h + 1],
                             jnp.where(lane == b0 + 1, mid[:, h:h + 1],
                                       jnp.where(lane == b0 + 2, lo[:, h:h + 1], 0.0)))
            qa_ref[0, h] = jnp.where(own, qn, ones).astype(BF16)
            ka_ref[0, h] = jnp.where(own, kn, bias).astype(BF16)
            vt_ref[0, h, 0, 0:HEAD_DIM, :] = v_t[half * HEAD_DIM:(half + 1) * HEAD_DIM]
            vt_ref[0, h, 0, HEAD_DIM:VT_ROWS, :] = ones_row


def _fox_prep(p384, fl, bf_row, qg, kg):
    B, S, _ = p384.shape
    tp = FOX_TK
    col = lambda j: pl.BlockSpec((1, tp, D_FOX), lambda b, i, j=j: (b, i, j))
    aug = jax.ShapeDtypeStruct((B, N_FOX_HEADS, S, LANES), BF16)
    aug_spec = pl.BlockSpec((1, N_FOX_HEADS, tp, LANES), lambda b, i: (b, 0, i, 0))
    vt = jax.ShapeDtypeStruct((B, N_FOX_HEADS, S // tp, VT_ROWS, tp), BF16)
    vt_spec = pl.BlockSpec((1, N_FOX_HEADS, 1, VT_ROWS, tp), lambda b, i: (b, 0, i, 0, 0))
    return pl.pallas_call(
        _foxprep_kernel,
        out_shape=(aug, aug, vt),
        grid=(B, S // tp),
        in_specs=[col(4), col(5), col(6),
                  pl.BlockSpec((1, tp, LANES), lambda b, i: (b, i, 0)),
                  pl.BlockSpec((1, LANES), lambda b, i: (0, 0)),
                  pl.BlockSpec((1, D_FOX), lambda b, i: (0, 0)),
                  pl.BlockSpec((1, D_FOX), lambda b, i: (0, 0))],
        out_specs=(aug_spec, aug_spec, vt_spec),
        scratch_shapes=[pltpu.VMEM((SUBLANES, LANES), F32)],
        compiler_params=_cparams("arbitrary", "arbitrary"),
        name="fox_prep",
    )(p384, p384, p384, fl, bf_row, qg, kg)


def _fox_kernel(q_ref, k_ref, vt_ref, o_ref, m_ref, acc_ref, s0_ref, s1_ref, mb0_ref, mb1_ref):
    s_refs = (s0_ref, s1_ref)
    mb_refs = (mb0_ref, mb1_ref)
    i = pl.program_id(2)
    n_qt = FOX_BQ // FOX_TQ
    units = [(hh, qt) for hh in range(2) for qt in range(n_qt)]
    n_u = len(units)
    dn_t = (((1,), (1,)), ((), ()))
    m_ref[...] = jnp.full(m_ref.shape, NEG_INF, F32)
    acc_ref[...] = jnp.zeros_like(acc_ref)

    def diag_keys(qt, half):
        d = qt * FOX_TQ - half * FOX_TK
        nk = min(max(d + FOX_TQ, 0), FOX_TK)
        return nk, (d if nk - 1 > d else None)

    def scores(j, slot, u):
        hh, qt = units[u]
        start = pl.multiple_of(j * FOX_TK, FOX_TK)
        k = k_ref[0, hh, pl.ds(start, FOX_TK), :]
        q = q_ref[0, hh, qt * FOX_TQ:(qt + 1) * FOX_TQ, :]
        s = lax.dot_general(k, q, dn_t, preferred_element_type=F32)
        s_refs[slot][u] = s
        mb_refs[slot][u] = jnp.max(s, axis=0, keepdims=True)

    def update(j, slot, u, diag_half=None):
        hh, qt = units[u]
        nk, mask_off = (FOX_TK, None) if diag_half is None else diag_keys(qt, diag_half)
        if nk == 0:
            return
        s = s_refs[slot][u, 0:nk, :]
        m_blk = mb_refs[slot][u]
        if mask_off is not None or nk < FOX_TK:
            if mask_off is not None:
                key = lax.broadcasted_iota(jnp.int32, s.shape, 0)
                qry = lax.broadcasted_iota(jnp.int32, s.shape, 1)
                s = jnp.where(key <= qry + mask_off, s, NEG_INF)
            m_blk = jnp.max(s, axis=0, keepdims=True)
        m_old = m_ref[hh, qt]
        m_new = jnp.maximum(m_old, m_blk)
        p = jnp.exp2((s - m_new).astype(BF16))
        vt = vt_ref[0, hh, j, :, 0:nk]
        acc_ref[hh, qt] = jnp.exp2(m_old - m_new) * acc_ref[hh, qt] + jnp.dot(
            vt, p, preferred_element_type=F32)
        m_ref[hh, qt] = m_new

    for u in range(n_u):
        scores(0, 0, u)

    def body(j, carry):
        for u in range(n_u):
            scores(2 * j + 1, 1, u)
            update(2 * j, 0, u)
        for u in range(n_u):
            scores(2 * j + 2, 0, u)
            update(2 * j + 1, 1, u)
        return carry

    lax.fori_loop(0, i, body, 0)
    for u in range(n_u):
        if diag_keys(units[u][1], 1)[0] > 0:
            scores(2 * i + 1, 1, u)
        update(2 * i, 0, u, 0)
    for u in range(n_u):
        update(2 * i + 1, 1, u, 1)

    for qt in range(n_qt):
        rows = []
        for hh in range(2):
            acc = acc_ref[hh, qt]
            rows.append(acc[0:HEAD_DIM] * (1.0 / acc[HEAD_DIM:HEAD_DIM + 1]))
        pair_t = jnp.concatenate(rows, axis=0)
        o_ref[0, qt * FOX_TQ:(qt + 1) * FOX_TQ, :] = pair_t.T.astype(BF16)


def _fox_attention(qa, ka, vt):
    B, H, S, _ = qa.shape
    n_qt = FOX_BQ // FOX_TQ
    stage = pltpu.VMEM((2 * n_qt, FOX_TK, FOX_TQ), F32)
    stage_max = pltpu.VMEM((2 * n_qt, 1, FOX_TQ), F32)
    return pl.pallas_call(
        _fox_kernel,
        out_shape=jax.ShapeDtypeStruct((B, S, D_FOX), BF16),
        grid=(B, H // 2, S // FOX_BQ),
        in_specs=[pl.BlockSpec((1, 2, FOX_BQ, LANES), lambda b, p, i: (b, p, i, 0)),
                  pl.BlockSpec((1, 2, S, LANES), lambda b, p, i: (b, p, 0, 0)),
                  pl.BlockSpec((1, 2, S // FOX_TK, VT_ROWS, FOX_TK),
                               lambda b, p, i: (b, p, 0, 0, 0))],
        out_specs=pl.BlockSpec((1, FOX_BQ, LANES), lambda b, p, i: (b, i, p)),
        scratch_shapes=[pltpu.VMEM((2, n_qt, 1, FOX_TQ), F32),
                        pltpu.VMEM((2, n_qt, VT_ROWS, FOX_TQ), F32),
                        stage, stage, stage_max, stage_max],
        compiler_params=_cparams("arbitrary", "arbitrary", "arbitrary"),
        name="fox_attention",
    )(qa, ka, vt)


FF_CHUNK = MXU_DIM


def _mix_ffn_kernel(yr_ref, yg_ref, yf_ref, x_ref, gt1_ref, sc_ref, sh_ref, gt2_ref, g_ref,
                    wo_ref, wu_ref, cw_ref, cb_ref, wd_ref, o_ref, carry_ref, stage_ref, act_ref):
    tm = x_ref.shape[1]

    @pl.when(pl.program_id(1) == 0)
    def _():
        carry_ref[...] = jnp.zeros_like(carry_ref)

    mix = jnp.concatenate([yr_ref[0], yg_ref[0], yf_ref[0]], axis=-1)
    xn = x_ref[0] + gt1_ref[0] * jnp.dot(mix, wo_ref[...], preferred_element_type=F32)
    o_ref[0] = xn
    h = _mod_rms(xn, g_ref[...], sc_ref[0], sh_ref[0]).astype(BF16)

    def conv_cols(c0):
        cs = slice(c0, c0 + FF_CHUNK)
        up = jnp.dot(h, wu_ref[:, cs], preferred_element_type=F32)
        stage_ref[0:SUBLANES, :] = carry_ref[:, cs]
        stage_ref[SUBLANES:, :] = up
        carry_ref[:, cs] = up[tm - SUBLANES:, :]
        prev1 = stage_ref[SUBLANES - 1:SUBLANES - 1 + tm, :]
        prev2 = stage_ref[SUBLANES - 2:SUBLANES - 2 + tm, :]
        return (cw_ref[2:3, cs] * up + cw_ref[1:2, cs] * prev1 + cw_ref[0:1, cs] * prev2
                + cb_ref[:, cs])

    for f in range(D_FF // FF_CHUNK):
        a = conv_cols(f * FF_CHUNK)
        b = conv_cols(D_FF + f * FF_CHUNK)
        act_ref[:, f * FF_CHUNK:(f + 1) * FF_CHUNK] = ((a * jax.nn.sigmoid(a)) * b).astype(BF16)
    down = jnp.dot(act_ref[...], wd_ref[...], preferred_element_type=F32)
    o_ref[0] = o_ref[0] + gt2_ref[0] * down


def _mix_ffn(yr, yg, yf, x, gt1, sc, sh, gt2, g, w_o, w_up, conv_w, conv_b, w_down, tm):
    B, S, D = x.shape
    row = lambda n: pl.BlockSpec((1, tm, n), lambda b, i: (b, i, 0))
    vec = pl.BlockSpec((1, 1, D), lambda b, i: (b, 0, 0))
    return pl.pallas_call(
        _mix_ffn_kernel,
        out_shape=jax.ShapeDtypeStruct((B, S, D), F32),
        grid=(B, S // tm),
        in_specs=[row(D_RET), row(D_GM), row(D_FOX), row(D), vec, vec, vec, vec,
                  pl.BlockSpec((1, D), lambda b, i: (0, 0)),
                  _resident(w_o.shape), _resident(w_up.shape), _resident(conv_w.shape),
                  _resident(conv_b.shape), _resident(w_down.shape)],
        out_specs=row(D),
        scratch_shapes=[pltpu.VMEM((SUBLANES, 2 * D_FF), F32),
                        pltpu.VMEM((tm + SUBLANES, FF_CHUNK), F32),
                        pltpu.VMEM((tm, D_FF), BF16)],
        compiler_params=_cparams("arbitrary", "arbitrary"),
        name="mix_ffn",
    )(yr, yg, yf, x, gt1, sc, sh, gt2, g, w_o, w_up, conv_w, conv_b, w_down)


def kernel(x, c, ada_w, ada_b, norm1_g, w_in, ret_norm_g, gm_ln_g, gm_ln_b, gm_ws, gm_bs,
           fox_qn_g, fox_kn_g, fox_bf, w_o, norm2_g, w_up, conv_w, conv_b, w_down):
    B, S, D = x.shape
    L = ada_w.shape[0]
    assert D == D_MODEL and S % FOX_BQ == 0
    tm = 512

    mod = _modulation(c, ada_w, ada_b).reshape(L, B, 6, 1, D)
    cos_t, sin_t, dq, dk = _rope_tables(S)

    for l in range(L):
        sh1, sc1, gt1, sh2, sc2, gt2 = (mod[l, :, i] for i in range(6))
        w_f = jnp.pad(w_in[l, :, N_MAIN:], ((0, 0), (0, LANES - N_FOX_HEADS))).astype(BF16)
        p384, pgm, fl = _in_projection(x, sc1, sh1, norm1_g[l].reshape(1, D), w_in, l, w_f,
                                       cos_t, sin_t, dq, dk, gm_ln_g[l].reshape(1, D_GM),
                                       gm_ln_b[l].reshape(1, D_GM), tm)

        y_r = _retention(p384, ret_norm_g[l].reshape(1, D_RET), tm)

        bias_tile = jnp.repeat(gm_bs[l].T, HEAD_DIM, axis=1)
        y_g = _spatial_gate(pgm, gm_ws[l].reshape(N_GM_HEADS * CHUNK, CHUNK), bias_tile, tm)

        bf_row = jnp.pad(fox_bf[l], (0, LANES - N_FOX_HEADS)).reshape(1, LANES)
        qa, ka, vt = _fox_prep(p384, fl, bf_row,
                               jnp.tile(fox_qn_g[l], N_FOX_HEADS).reshape(1, D_FOX),
                               jnp.tile(fox_kn_g[l], N_FOX_HEADS).reshape(1, D_FOX))
        y_f = _fox_attention(qa, ka, vt)

        x = _mix_ffn(y_r, y_g, y_f, x, gt1, sc2, sh2, gt2, norm2_g[l].reshape(1, D),
                     w_o[l].astype(BF16), w_up[l].astype(BF16), conv_w[l],
                     conv_b[l].reshape(1, 2 * D_FF), w_down[l].astype(BF16), tm)
    return x
```

```python
import math

import jax
import jax.numpy as jnp
from jax import lax
from jax.experimental import pallas as pl
from jax.experimental.pallas import tpu as pltpu

D_MODEL = 1024
HEAD_DIM = 64
N_RET_HEADS = 6
N_GM_HEADS = 4
N_FOX_HEADS = 6
D_RET = N_RET_HEADS * HEAD_DIM
D_GM = N_GM_HEADS * HEAD_DIM
D_FOX = N_FOX_HEADS * HEAD_DIM
CHUNK = 128
D_FF = 2816
ROPE_BASE = 10000.0
EPS = 1e-6
NEG_INF = -1e30

LANES = 128
SUBLANES = 8
MXU_DIM = 256
VMEM_LIMIT = 52 * 1024 * 1024

N_MAIN = 4 * D_RET + 2 * D_GM + 3 * D_FOX
GM_COL0 = 4 * D_RET
FOX_COL0 = GM_COL0 + 2 * D_GM
N_IN = N_MAIN + N_FOX_HEADS
FL_LANE0 = LANES - N_FOX_HEADS

LOG2E = math.log2(math.e)
FOX_TK = 512
FOX_TQ = MXU_DIM
FOX_BQ = 2 * FOX_TK
VT_ROWS = 80

F32 = jnp.float32
BF16 = jnp.bfloat16


def _cparams(*sem):
    return pltpu.CompilerParams(dimension_semantics=sem, vmem_limit_bytes=VMEM_LIMIT)


def _layer_resident(stacked, layer):
    nd = stacked.ndim - 1
    return pl.BlockSpec((1,) + stacked.shape[1:], lambda *_: (layer,) + (0,) * nd,
                        pipeline_mode=pl.Buffered(1))


def _lane_iota(shape):
    return lax.broadcasted_iota(jnp.int32, shape, len(shape) - 1)


def _mod_kernel(c_ref, w_ref, b_ref, o_ref):
    c = c_ref[...]
    cond = c * jax.nn.sigmoid(c)
    o_ref[0] = jnp.dot(cond, w_ref[0], precision=lax.Precision.HIGHEST,
                       preferred_element_type=F32) + b_ref[0]


def _modulation(c, ada_w, ada_b):
    L, D, N = ada_w.shape
    B = c.shape[0]
    tn = 1536
    return pl.pallas_call(
        _mod_kernel,
        out_shape=jax.ShapeDtypeStruct((L, B, N), F32),
        grid=(L, N // tn),
        in_specs=[pl.BlockSpec((B, D), lambda l, j: (0, 0)),
                  pl.BlockSpec((1, D, tn), lambda l, j: (l, 0, j)),
                  pl.BlockSpec((1, 1, tn), lambda l, j: (l, 0, j))],
        out_specs=pl.BlockSpec((1, B, tn), lambda l, j: (l, 0, j)),
        compiler_params=_cparams("arbitrary", "arbitrary"),
        name="adaln_mod",
    )(c, ada_w, ada_b.reshape(L, 1, N))


def _ret_log_gamma(h):
    return math.log(1.0 - 2.0 ** (-5.0 - h))


def _rope_kernel(inv_ref, cos_ref, sin_ref, dq_ref, dk_ref):
    ts = cos_ref.shape[0]
    pos = (lax.broadcasted_iota(jnp.int32, (ts, LANES), 0) + pl.program_id(0) * ts).astype(F32)
    ang = pos * inv_ref[...]
    first_half = (_lane_iota((ts, LANES)) % HEAD_DIM) < (HEAD_DIM // 2)
    cos_ref[...] = jnp.cos(ang)
    s = jnp.sin(ang)
    sin_ref[...] = jnp.where(first_half, -s, s)

    head = _lane_iota((CHUNK, D_RET)) // HEAD_DIM
    lg = jnp.zeros((CHUNK, D_RET), F32)
    for h in range(N_RET_HEADS):
        lg = jnp.where(head == h, _ret_log_gamma(h), lg)
    t1 = lax.broadcasted_iota(jnp.int32, (CHUNK, D_RET), 0).astype(F32) + 1.0
    dq_ref[...] = jnp.exp(lg * t1)
    dk_ref[...] = jnp.exp(-lg * t1) * (HEAD_DIM ** -0.5)


def _rope_tables(S):
    half = HEAD_DIM // 2
    inv = ROPE_BASE ** (-jnp.arange(half, dtype=F32) / half)
    inv_row = jnp.tile(inv, LANES // half).reshape(1, LANES)
    ts = 512
    decay = jax.ShapeDtypeStruct((CHUNK, D_RET), F32)
    decay_spec = pl.BlockSpec((CHUNK, D_RET), lambda i: (0, 0))
    return pl.pallas_call(
        _rope_kernel,
        out_shape=(jax.ShapeDtypeStruct((S, LANES), F32),) * 2 + (decay, decay),
        grid=(S // ts,),
        in_specs=[pl.BlockSpec((1, LANES), lambda i: (0, 0))],
        out_specs=(pl.BlockSpec((ts, LANES), lambda i: (i, 0)),) * 2 + (decay_spec, decay_spec),
        compiler_params=_cparams("arbitrary"),
        name="rope_tables",
    )(inv_row)


def _mod_rms(x, g, sc, sh):
    ms = jnp.mean(x * x, axis=-1, keepdims=True)
    return (x * lax.rsqrt(ms + EPS) * g) * (1.0 + sc) + sh


def _swap_half_heads(x):
    first_half = (_lane_iota(x.shape) % HEAD_DIM) < (HEAD_DIM // 2)
    return jnp.where(first_half, pltpu.roll(x, LANES - HEAD_DIM // 2, 1),
                     pltpu.roll(x, HEAD_DIM // 2, 1))


def _gelu(x):
    return 0.5 * x * (1.0 + jnp.tanh(math.sqrt(2.0 / math.pi) * (x + 0.044715 * (x * x * x))))


def _pair_rms_scale(x):
    low = _lane_iota(x.shape) < HEAD_DIM
    sq = x * x
    ss0 = jnp.sum(jnp.where(low, sq, 0.0), axis=-1, keepdims=True)
    ss1 = jnp.sum(jnp.where(low, 0.0, sq), axis=-1, keepdims=True)
    return jnp.where(low, lax.rsqrt(ss0 * (1.0 / HEAD_DIM) + EPS),
                     lax.rsqrt(ss1 * (1.0 / HEAD_DIM) + EPS))


def _inproj_kernel(x_ref, sc_ref, sh_ref, g_ref, w_ref, cos_ref, sin_ref, dq_ref, dk_ref,
                   lng_ref, lnb_ref, ws_ref, gb_ref, bf_ref, qg_ref, kg_ref,
                   pret_ref, yg_ref, qa_ref, ka_ref, vt_ref, carry_ref):
    tm = x_ref.shape[1]
    h = _mod_rms(x_ref[0], g_ref[...], sc_ref[0], sh_ref[0]).astype(BF16)

    @pl.when(pl.program_id(1) == 0)
    def _():
        carry_ref[...] = jnp.zeros_like(carry_ref)

    w_tail = w_ref[0, :, N_IN - LANES:N_IN].astype(BF16)
    z = jnp.dot(h, w_tail, preferred_element_type=F32) + bf_ref[...]
    logf = jnp.minimum(z, 0.0) - jnp.log1p(jnp.exp(-jnp.abs(z)))
    tri = (lax.broadcasted_iota(jnp.int32, (CHUNK, CHUNK), 1) <=
           lax.broadcasted_iota(jnp.int32, (CHUNK, CHUNK), 0)).astype(F32)
    carry = carry_ref[0:1, :]
    parts = []
    for c in range(tm // CHUNK):
        cum = jnp.dot(tri, logf[c * CHUNK:(c + 1) * CHUNK], precision=lax.Precision.HIGHEST,
                      preferred_element_type=F32) + carry
        carry = cum[CHUNK - 1:CHUNK, :]
        parts.append(cum)
    carry_ref[...] = jnp.broadcast_to(carry, carry_ref.shape)
    neg_cum = jnp.concatenate(parts, axis=0) * (-LOG2E)
    bias_hi = neg_cum.astype(BF16).astype(F32)
    rest = neg_cum - bias_hi
    bias_mid = rest.astype(BF16).astype(F32)
    bias_lo = (rest - bias_mid).astype(BF16).astype(F32)

    lane = _lane_iota((tm, LANES))
    low = lane < HEAD_DIM
    ones_row = jnp.where(lax.broadcasted_iota(jnp.int32, (VT_ROWS - HEAD_DIM, tm), 0) == 0,
                         1.0, 0.0).astype(BF16)

    def fox_tiles(piece, kind, pair):
        if kind == 2:
            v_t = piece.T.astype(BF16)
        else:
            cols = slice(pair * LANES, (pair + 1) * LANES)
            gain = qg_ref[:, cols] * (HEAD_DIM ** -0.5 * LOG2E) if kind == 0 else kg_ref[:, cols]
            normed = piece * _pair_rms_scale(piece) * gain
        for half in range(2):
            hd = 2 * pair + half
            own = low if half == 0 else jnp.logical_not(low)
            b0 = HEAD_DIM * (1 - half)
            if kind == 0:
                ones = jnp.where((lane >= b0) & (lane < b0 + 3), 1.0, 0.0)
                qa_ref[0, hd] = jnp.where(own, normed, ones).astype(BF16)
            elif kind == 1:
                col = slice(FL_LANE0 + hd, FL_LANE0 + hd + 1)
                bias = jnp.where(lane == b0, bias_hi[:, col],
                                 jnp.where(lane == b0 + 1, bias_mid[:, col],
                                           jnp.where(lane == b0 + 2, bias_lo[:, col], 0.0)))
                ka_ref[0, hd] = jnp.where(own, normed, bias).astype(BF16)
            else:
                vt_ref[0, hd, 0, 0:HEAD_DIM, :] = v_t[half * HEAD_DIM:(half + 1) * HEAD_DIM]
                vt_ref[0, hd, 0, HEAD_DIM:VT_ROWS, :] = ones_row

    def spatial_gate(u, vn):
        lane_c = _lane_iota((CHUNK, LANES))
        keep_a = jnp.where(lane_c < HEAD_DIM, 1.0, 0.0).astype(BF16)
        keep_b = jnp.where(lane_c < HEAD_DIM, 0.0, 1.0).astype(BF16)
        row = lax.broadcasted_iota(jnp.int32, (CHUNK, 2 * CHUNK), 0)
        col = _lane_iota((CHUNK, 2 * CHUNK)) % CHUNK
        vb = vn.astype(BF16)
        for pair in range(N_GM_HEADS // 2):
            w_pair = jnp.where(col <= row, ws_ref[:, pair * 2 * CHUNK:(pair + 1) * 2 * CHUNK],
                               0.0).astype(BF16)
            ls = slice(pair * LANES, (pair + 1) * LANES)
            for c in range(tm // CHUNK):
                rs = slice(c * CHUNK, (c + 1) * CHUNK)
                v_tile = vb[rs, ls]
                stacked = jnp.concatenate([v_tile * keep_a, v_tile * keep_b], axis=0)
                mixed = jnp.dot(w_pair, stacked, preferred_element_type=F32)
                yg_ref[0, rs, ls] = (u[rs, ls] * (mixed + gb_ref[:, ls])).astype(BF16)

    def rotary_decay(piece, table_ref, tile):
        rot = piece * cos_ref[...] + _swap_half_heads(piece) * sin_ref[...]
        table = table_ref[:, tile * LANES:(tile + 1) * LANES]
        return rot * jnp.concatenate([table] * (tm // CHUNK), axis=0)

    for lo in range(0, N_MAIN, MXU_DIM):
        width = min(MXU_DIM, N_MAIN - lo)
        r = jnp.dot(h, w_ref[0, :, lo:lo + width].astype(BF16), preferred_element_type=F32)
        if lo == GM_COL0:
            gate_u = _gelu(r)
            continue
        if lo == GM_COL0 + D_GM:
            v = _gelu(r)
            mu = jnp.mean(v, axis=-1, keepdims=True)
            vc = v - mu
            var = jnp.mean(vc * vc, axis=-1, keepdims=True)
            spatial_gate(gate_u, vc * lax.rsqrt(var + EPS) * lng_ref[...] + lnb_ref[...])
            continue
        for off in range(0, width, LANES):
            c0 = lo + off
            piece = r[:, off:off + LANES]
            if c0 < D_RET:
                piece = rotary_decay(piece, dq_ref, c0 // LANES)
            elif c0 < 2 * D_RET:
                piece = rotary_decay(piece, dk_ref, (c0 - D_RET) // LANES)
            elif 3 * D_RET <= c0 < 4 * D_RET:
                piece = piece * jax.nn.sigmoid(piece)
            if c0 < GM_COL0:
                pret_ref[0, :, c0:c0 + LANES] = piece.astype(BF16)
            else:
                kind, pair = divmod((c0 - FOX_COL0) // LANES, N_FOX_HEADS // 2)
                fox_tiles(piece, kind, pair)


def _in_projection(x, sc, sh, g, w_in, layer, cos_t, sin_t, dq, dk, ln_g, ln_b, ws_cat,
                   gate_bias, bf_row, qg, kg, tm):
    B, S, D = x.shape
    assert tm == FOX_TK
    const = lambda shape: pl.BlockSpec(shape, lambda b, i: (0,) * len(shape))
    aug = jax.ShapeDtypeStruct((B, N_FOX_HEADS, S, LANES), BF16)
    aug_spec = pl.BlockSpec((1, N_FOX_HEADS, tm, LANES), lambda b, i: (b, 0, i, 0))
    return pl.pallas_call(
        _inproj_kernel,
        out_shape=(jax.ShapeDtypeStruct((B, S, 4 * D_RET), BF16),
                   jax.ShapeDtypeStruct((B, S, D_GM), BF16),
                   aug, aug,
                   jax.ShapeDtypeStruct((B, N_FOX_HEADS, S // tm, VT_ROWS, tm), BF16)),
        grid=(B, S // tm),
        in_specs=[pl.BlockSpec((1, tm, D), lambda b, i: (b, i, 0)),
                  pl.BlockSpec((1, 1, D), lambda b, i: (b, 0, 0)),
                  pl.BlockSpec((1, 1, D), lambda b, i: (b, 0, 0)),
                  const((1, D)),
                  _layer_resident(w_in, layer),
                  pl.BlockSpec((tm, LANES), lambda b, i: (i, 0)),
                  pl.BlockSpec((tm, LANES), lambda b, i: (i, 0)),
                  const((CHUNK, D_RET)), const((CHUNK, D_RET)),
                  const((1, D_GM)), const((1, D_GM)),
                  const((CHUNK, N_GM_HEADS * CHUNK)), const((CHUNK, D_GM)),
                  const((1, LANES)), const((1, D_FOX)), const((1, D_FOX))],
        out_specs=(pl.BlockSpec((1, tm, 4 * D_RET), lambda b, i: (b, i, 0)),
                   pl.BlockSpec((1, tm, D_GM), lambda b, i: (b, i, 0)),
                   aug_spec, aug_spec,
                   pl.BlockSpec((1, N_FOX_HEADS, 1, VT_ROWS, tm), lambda b, i: (b, 0, i, 0, 0))),
        scratch_shapes=[pltpu.VMEM((SUBLANES, LANES), F32)],
        compiler_params=_cparams("arbitrary", "arbitrary"),
        name="in_proj",
    )(x, sc, sh, g, w_in, cos_t, sin_t, dq, dk, ln_g, ln_b, ws_cat, gate_bias, bf_row, qg, kg)


def _ret_kernel(q_ref, k_ref, v_ref, g_ref, ng_ref, o_ref, state_ref):
    tr = q_ref.shape[1]
    n_pairs = N_RET_HEADS // 2
    low = _lane_iota((CHUNK, LANES)) < HEAD_DIM
    keep0 = jnp.where(low, 1.0, 0.0).astype(BF16)
    keep1 = jnp.where(low, 0.0, 1.0).astype(BF16)
    causal = (lax.broadcasted_iota(jnp.int32, (CHUNK, CHUNK), 0) >=
              lax.broadcasted_iota(jnp.int32, (CHUNK, CHUNK), 1))
    blockdiag = (lax.broadcasted_iota(jnp.int32, (LANES, LANES), 0) < HEAD_DIM) == low
    dn_t = (((1,), (1,)), ((), ()))
    dn_ta = (((0,), (0,)), ((), ()))

    @pl.when(pl.program_id(1) == 0)
    def _():
        state_ref[...] = jnp.zeros_like(state_ref)

    n_c = tr // CHUNK
    tiles = [(p, c) for p in range(n_pairs) for c in range(n_c)]
    sl = lambda p, c: (slice(c * CHUNK, (c + 1) * CHUNK), slice(p * LANES, (p + 1) * LANES))
    kvs, scores = {}, {}
    for p, c in tiles:
        rs, cs = sl(p, c)
        q, k, v = q_ref[0, rs, cs], k_ref[0, rs, cs], v_ref[0, rs, cs]
        kvs[p, c] = lax.dot_general(k, v, dn_ta, preferred_element_type=F32)
        for hh, keep in enumerate((keep0, keep1)):
            s = lax.dot_general(q * keep, k, dn_t, preferred_element_type=F32)
            scores[p, c, hh] = jnp.where(causal, s, 0.0).astype(BF16)
    states = {}
    for p in range(n_pairs):
        chunk_decay = jnp.where(low[0:1, :], math.exp(_ret_log_gamma(2 * p) * CHUNK),
                                math.exp(_ret_log_gamma(2 * p + 1) * CHUNK))
        state = state_ref[p]
        for c in range(n_c):
            states[p, c] = state.astype(BF16)
            state = (state + jnp.where(blockdiag, kvs[p, c], 0.0)) * chunk_decay
        state_ref[p] = state
    for p, c in tiles:
        rs, cs = sl(p, c)
        q = q_ref[0, rs, cs]
        rhs = jnp.concatenate([v_ref[0, rs, cs], states[p, c]], axis=0)
        ys = [jnp.dot(jnp.concatenate([scores[p, c, hh], q * keep], axis=1), rhs,
                      preferred_element_type=F32) for hh, keep in enumerate((keep0, keep1))]
        y = jnp.where(low, ys[0], ys[1])
        yn = y * _pair_rms_scale(y) * ng_ref[:, cs]
        o_ref[0, rs, cs] = (yn * g_ref[0, rs, cs].astype(F32)).astype(BF16)


def _retention(p_ret, ng, tr):
    B, S, _ = p_ret.shape
    col = lambda j: pl.BlockSpec((1, tr, D_RET), lambda b, i, j=j: (b, i, j))
    return pl.pallas_call(
        _ret_kernel,
        out_shape=jax.ShapeDtypeStruct((B, S, D_RET), BF16),
        grid=(B, S // tr),
        in_specs=[col(0), col(1), col(2), col(3),
                  pl.BlockSpec((1, D_RET), lambda b, i: (0, 0))],
        out_specs=pl.BlockSpec((1, tr, D_RET), lambda b, i: (b, i, 0)),
        scratch_shapes=[pltpu.VMEM((N_RET_HEADS // 2, LANES, LANES), F32)],
        compiler_params=_cparams("arbitrary", "arbitrary"),
        name="retention",
    )(p_ret, p_ret, p_ret, p_ret, ng)


def _fox_kernel(q_ref, k_ref, vt_ref, o_ref, m_ref, acc_ref, s0_ref, s1_ref, mb0_ref, mb1_ref):
    s_refs = (s0_ref, s1_ref)
    mb_refs = (mb0_ref, mb1_ref)
    i = pl.program_id(2)
    n_qt = FOX_BQ // FOX_TQ
    units = [(hh, qt) for hh in range(2) for qt in range(n_qt)]
    n_u = len(units)
    dn_t = (((1,), (1,)), ((), ()))
    m_ref[...] = jnp.full(m_ref.shape, NEG_INF, F32)
    acc_ref[...] = jnp.zeros_like(acc_ref)

    def diag_keys(qt, half):
        d = qt * FOX_TQ - half * FOX_TK
        nk = min(max(d + FOX_TQ, 0), FOX_TK)
        return nk, (d if nk - 1 > d else None)

    def scores(j, slot, u):
        hh, qt = units[u]
        start = pl.multiple_of(j * FOX_TK, FOX_TK)
        k = k_ref[0, hh, pl.ds(start, FOX_TK), :]
        q = q_ref[0, hh, qt * FOX_TQ:(qt + 1) * FOX_TQ, :]
        s = lax.dot_general(k, q, dn_t, preferred_element_type=F32)
        s_refs[slot][u] = s
        mb_refs[slot][u] = jnp.max(s, axis=0, keepdims=True)

    def update(j, slot, u, diag_half=None):
        hh, qt = units[u]
        nk, mask_off = (FOX_TK, None) if diag_half is None else diag_keys(qt, diag_half)
        if nk == 0:
            return
        s = s_refs[slot][u, 0:nk, :]
        m_blk = mb_refs[slot][u]
        if mask_off is not None or nk < FOX_TK:
            if mask_off is not None:
                key = lax.broadcasted_iota(jnp.int32, s.shape, 0)
                qry = lax.broadcasted_iota(jnp.int32, s.shape, 1)
                s = jnp.where(key <= qry + mask_off, s, NEG_INF)
            m_blk = jnp.max(s, axis=0, keepdims=True)
        m_old = m_ref[hh, qt]
        m_new = jnp.maximum(m_old, m_blk)
        p = jnp.exp2((s - m_new).astype(BF16))
        vt = vt_ref[0, hh, j, :, 0:nk]
        acc_ref[hh, qt] = jnp.exp2(m_old - m_new) * acc_ref[hh, qt] + jnp.dot(
            vt, p, preferred_element_type=F32)
        m_ref[hh, qt] = m_new

    for u in range(n_u):
        scores(0, 0, u)

    def body(j, carry):
        for u in range(n_u):
            scores(2 * j + 1, 1, u)
            update(2 * j, 0, u)
        for u in range(n_u):
            scores(2 * j + 2, 0, u)
            update(2 * j + 1, 1, u)
        return carry

    lax.fori_loop(0, i, body, 0)
    for u in range(n_u):
        if diag_keys(units[u][1], 1)[0] > 0:
            scores(2 * i + 1, 1, u)
        update(2 * i, 0, u, 0)
    for u in range(n_u):
        update(2 * i + 1, 1, u, 1)

    for qt in range(n_qt):
        rows = []
        for hh in range(2):
            acc = acc_ref[hh, qt]
            rows.append(acc[0:HEAD_DIM] * (1.0 / acc[HEAD_DIM:HEAD_DIM + 1]))
        pair_t = jnp.concatenate(rows, axis=0)
        o_ref[0, qt * FOX_TQ:(qt + 1) * FOX_TQ, :] = pair_t.T.astype(BF16)


def _fox_attention(qa, ka, vt):
    B, H, S, _ = qa.shape
    n_qt = FOX_BQ // FOX_TQ
    stage = pltpu.VMEM((2 * n_qt, FOX_TK, FOX_TQ), F32)
    stage_max = pltpu.VMEM((2 * n_qt, 1, FOX_TQ), F32)
    return pl.pallas_call(
        _fox_kernel,
        out_shape=jax.ShapeDtypeStruct((B, S, D_FOX), BF16),
        grid=(B, H // 2, S // FOX_BQ),
        in_specs=[pl.BlockSpec((1, 2, FOX_BQ, LANES), lambda b, p, i: (b, p, i, 0)),
                  pl.BlockSpec((1, 2, S, LANES), lambda b, p, i: (b, p, 0, 0)),
                  pl.BlockSpec((1, 2, S // FOX_TK, VT_ROWS, FOX_TK),
                               lambda b, p, i: (b, p, 0, 0, 0))],
        out_specs=pl.BlockSpec((1, FOX_BQ, LANES), lambda b, p, i: (b, i, p)),
        scratch_shapes=[pltpu.VMEM((2, n_qt, 1, FOX_TQ), F32),
                        pltpu.VMEM((2, n_qt, VT_ROWS, FOX_TQ), F32),
                        stage, stage, stage_max, stage_max],
        compiler_params=_cparams("arbitrary", "arbitrary", "arbitrary"),
        name="fox_attention",
    )(qa, ka, vt)


FF_CHUNK = MXU_DIM


def _mix_ffn_kernel(yr_ref, yg_ref, yf_ref, x_ref, gt1_ref, sc_ref, sh_ref, gt2_ref, g_ref,
                    wo_ref, wu_ref, cw_ref, cb_ref, wd_ref, o_ref, carry_ref, stage_ref, act_ref):
    tm = x_ref.shape[1]

    @pl.when(pl.program_id(1) == 0)
    def _():
        carry_ref[...] = jnp.zeros_like(carry_ref)

    halves = []
    for rs in (slice(0, tm // 2), slice(tm // 2, tm)):
        mix = jnp.concatenate([yr_ref[0, rs], yg_ref[0, rs], yf_ref[0, rs]], axis=-1)
        xn = x_ref[0, rs] + gt1_ref[0] * jnp.dot(mix, wo_ref[0], preferred_element_type=F32)
        o_ref[0, rs] = xn
        halves.append(xn)
    h = jnp.concatenate([_mod_rms(xn, g_ref[...], sc_ref[0], sh_ref[0]).astype(BF16)
                         for xn in halves], axis=0)

    def conv_cols(c0):
        cs = slice(c0, c0 + FF_CHUNK)
        up = jnp.dot(h, wu_ref[0, :, cs], preferred_element_type=F32)
        stage_ref[0:SUBLANES, :] = carry_ref[:, cs]
        stage_ref[SUBLANES:, :] = up
        carry_ref[:, cs] = up[tm - SUBLANES:, :]
        prev1 = stage_ref[SUBLANES - 1:SUBLANES - 1 + tm, :]
        prev2 = stage_ref[SUBLANES - 2:SUBLANES - 2 + tm, :]
        return (cw_ref[0, 2:3, cs] * up + cw_ref[0, 1:2, cs] * prev1 + cw_ref[0, 0:1, cs] * prev2
                + cb_ref[0, :, cs])

    for f in range(D_FF // FF_CHUNK):
        a = conv_cols(f * FF_CHUNK)
        b = conv_cols(D_FF + f * FF_CHUNK)
        act_ref[:, f * FF_CHUNK:(f + 1) * FF_CHUNK] = ((a * jax.nn.sigmoid(a)) * b).astype(BF16)
    down = jnp.dot(act_ref[...], wd_ref[0], preferred_element_type=F32)
    o_ref[0] = o_ref[0] + gt2_ref[0] * down


def _mix_ffn(yr, yg, yf, x, gt1, sc, sh, gt2, g, layer, w_o, w_up, conv_w, conv_b, w_down, tm):
    B, S, D = x.shape
    row = lambda n: pl.BlockSpec((1, tm, n), lambda b, i: (b, i, 0))
    vec = pl.BlockSpec((1, 1, D), lambda b, i: (b, 0, 0))
    return pl.pallas_call(
        _mix_ffn_kernel,
        out_shape=jax.ShapeDtypeStruct((B, S, D), F32),
        grid=(B, S // tm),
        in_specs=[row(D_RET), row(D_GM), row(D_FOX), row(D), vec, vec, vec, vec,
                  pl.BlockSpec((1, D), lambda b, i: (0, 0)),
                  _layer_resident(w_o, layer), _layer_resident(w_up, layer),
                  _layer_resident(conv_w, layer), _layer_resident(conv_b, layer),
                  _layer_resident(w_down, layer)],
        out_specs=row(D),
        scratch_shapes=[pltpu.VMEM((SUBLANES, 2 * D_FF), F32),
                        pltpu.VMEM((tm + SUBLANES, FF_CHUNK), F32),
                        pltpu.VMEM((tm, D_FF), BF16)],
        compiler_params=_cparams("arbitrary", "arbitrary"),
        name="mix_ffn",
    )(yr, yg, yf, x, gt1, sc, sh, gt2, g, w_o, w_up, conv_w, conv_b, w_down)


def kernel(x, c, ada_w, ada_b, norm1_g, w_in, ret_norm_g, gm_ln_g, gm_ln_b, gm_ws, gm_bs,
           fox_qn_g, fox_kn_g, fox_bf, w_o, norm2_g, w_up, conv_w, conv_b, w_down):
    B, S, D = x.shape
    L = ada_w.shape[0]
    assert D == D_MODEL and S % FOX_BQ == 0
    tm = 512

    mod = _modulation(c, ada_w, ada_b).reshape(L, B, 6, 1, D)
    cos_t, sin_t, dq, dk = _rope_tables(S)
    w_o_b, w_up_b, w_down_b = w_o.astype(BF16), w_up.astype(BF16), w_down.astype(BF16)
    conv_b3 = conv_b.reshape(L, 1, 2 * D_FF)

    for l in range(L):
        sh1, sc1, gt1, sh2, sc2, gt2 = (mod[l, :, i] for i in range(6))
        ws_cat = gm_ws[l].transpose(1, 0, 2).reshape(CHUNK, N_GM_HEADS * CHUNK)
        gate_bias = jnp.repeat(gm_bs[l].T, HEAD_DIM, axis=1)
        bf_row = jnp.pad(fox_bf[l], (FL_LANE0, 0)).reshape(1, LANES)
        p_ret, y_g, qa, ka, vt = _in_projection(
            x, sc1, sh1, norm1_g[l].reshape(1, D), w_in, l, cos_t, sin_t, dq, dk,
            gm_ln_g[l].reshape(1, D_GM), gm_ln_b[l].reshape(1, D_GM), ws_cat, gate_bias, bf_row,
            jnp.tile(fox_qn_g[l], N_FOX_HEADS).reshape(1, D_FOX),
            jnp.tile(fox_kn_g[l], N_FOX_HEADS).reshape(1, D_FOX), tm)

        y_r = _retention(p_ret, ret_norm_g[l].reshape(1, D_RET), tm)
        y_f = _fox_attention(qa, ka, vt)

        x = _mix_ffn(y_r, y_g, y_f, x, gt1, sc2, sh2, gt2, norm2_g[l].reshape(1, D), l,
                     w_o_b, w_up_b, conv_w, conv_b3, w_down_b, tm)
    return x
```

```python
import math

import jax
import jax.numpy as jnp
from jax import lax
from jax.experimental import pallas as pl
from jax.experimental.pallas import tpu as pltpu

D_MODEL = 1024
HEAD_DIM = 64
N_RET_HEADS = 6
N_GM_HEADS = 4
N_FOX_HEADS = 6
D_RET = N_RET_HEADS * HEAD_DIM
D_GM = N_GM_HEADS * HEAD_DIM
D_FOX = N_FOX_HEADS * HEAD_DIM
CHUNK = 128
D_FF = 2816
ROPE_BASE = 10000.0
EPS = 1e-6
NEG_INF = -1e30

LANES = 128
SUBLANES = 8
MXU_DIM = 256
VMEM_LIMIT = 52 * 1024 * 1024

N_MAIN = 4 * D_RET + 2 * D_GM + 3 * D_FOX
GM_COL0 = 4 * D_RET
FOX_COL0 = GM_COL0 + 2 * D_GM
N_IN = N_MAIN + N_FOX_HEADS
FL_LANE0 = LANES - N_FOX_HEADS

LOG2E = math.log2(math.e)
FOX_TK = 512
FOX_TQ = MXU_DIM
FOX_BQ = 2 * FOX_TK
VT_ROWS = 80

F32 = jnp.float32
BF16 = jnp.bfloat16


def _cparams(*sem):
    return pltpu.CompilerParams(dimension_semantics=sem, vmem_limit_bytes=VMEM_LIMIT)


def _resident(whole):
    nd = whole.ndim
    return pl.BlockSpec(whole.shape, lambda *_: (0,) * nd, pipeline_mode=pl.Buffered(1))


def _layer_resident(stacked, layer):
    nd = stacked.ndim - 1
    return pl.BlockSpec((1,) + stacked.shape[1:], lambda *_: (layer,) + (0,) * nd,
                        pipeline_mode=pl.Buffered(1))


def _lane_iota(shape):
    return lax.broadcasted_iota(jnp.int32, shape, len(shape) - 1)


def _mod_kernel(c_ref, w_ref, b_ref, o_ref):
    c = c_ref[...]
    cond = c * jax.nn.sigmoid(c)
    o_ref[0] = jnp.dot(cond, w_ref[0], precision=lax.Precision.HIGHEST,
                       preferred_element_type=F32) + b_ref[0]


def _modulation(c, ada_w, ada_b):
    L, D, N = ada_w.shape
    B = c.shape[0]
    tn = 1536
    return pl.pallas_call(
        _mod_kernel,
        out_shape=jax.ShapeDtypeStruct((L, B, N), F32),
        grid=(L, N // tn),
        in_specs=[pl.BlockSpec((B, D), lambda l, j: (0, 0)),
                  pl.BlockSpec((1, D, tn), lambda l, j: (l, 0, j)),
                  pl.BlockSpec((1, 1, tn), lambda l, j: (l, 0, j))],
        out_specs=pl.BlockSpec((1, B, tn), lambda l, j: (l, 0, j)),
        compiler_params=_cparams("arbitrary", "arbitrary"),
        name="adaln_mod",
    )(c, ada_w, ada_b.reshape(L, 1, N))


def _ret_log_gamma(h):
    return math.log(1.0 - 2.0 ** (-5.0 - h))


def _rope_kernel(inv_ref, cos_ref, sin_ref, dq_ref, dk_ref):
    ts = cos_ref.shape[0]
    pos = (lax.broadcasted_iota(jnp.int32, (ts, LANES), 0) + pl.program_id(0) * ts).astype(F32)
    ang = pos * inv_ref[...]
    first_half = (_lane_iota((ts, LANES)) % HEAD_DIM) < (HEAD_DIM // 2)
    cos_ref[...] = jnp.cos(ang)
    s = jnp.sin(ang)
    sin_ref[...] = jnp.where(first_half, -s, s)

    head = _lane_iota((CHUNK, D_RET)) // HEAD_DIM
    lg = jnp.zeros((CHUNK, D_RET), F32)
    for h in range(N_RET_HEADS):
        lg = jnp.where(head == h, _ret_log_gamma(h), lg)
    t1 = lax.broadcasted_iota(jnp.int32, (CHUNK, D_RET), 0).astype(F32) + 1.0
    dq_ref[...] = jnp.exp(lg * t1)
    dk_ref[...] = jnp.exp(-lg * t1) * (HEAD_DIM ** -0.5)


def _rope_tables(S):
    half = HEAD_DIM // 2
    inv = ROPE_BASE ** (-jnp.arange(half, dtype=F32) / half)
    inv_row = jnp.tile(inv, LANES // half).reshape(1, LANES)
    ts = 512
    decay = jax.ShapeDtypeStruct((CHUNK, D_RET), F32)
    decay_spec = pl.BlockSpec((CHUNK, D_RET), lambda i: (0, 0))
    return pl.pallas_call(
        _rope_kernel,
        out_shape=(jax.ShapeDtypeStruct((S, LANES), F32),) * 2 + (decay, decay),
        grid=(S // ts,),
        in_specs=[pl.BlockSpec((1, LANES), lambda i: (0, 0))],
        out_specs=(pl.BlockSpec((ts, LANES), lambda i: (i, 0)),) * 2 + (decay_spec, decay_spec),
        compiler_params=_cparams("arbitrary"),
        name="rope_tables",
    )(inv_row)


def _mod_rms(x, g, sc, sh):
    ms = jnp.mean(x * x, axis=-1, keepdims=True)
    return (x * lax.rsqrt(ms + EPS) * g) * (1.0 + sc) + sh


def _swap_half_heads(x):
    first_half = (_lane_iota(x.shape) % HEAD_DIM) < (HEAD_DIM // 2)
    return jnp.where(first_half, pltpu.roll(x, LANES - HEAD_DIM // 2, 1),
                     pltpu.roll(x, HEAD_DIM // 2, 1))


def _gelu(x):
    return 0.5 * x * (1.0 + jnp.tanh(math.sqrt(2.0 / math.pi) * (x + 0.044715 * (x * x * x))))


def _pair_rms_scale(x):
    low = _lane_iota(x.shape) < HEAD_DIM
    sq = x * x
    ss0 = jnp.sum(jnp.where(low, sq, 0.0), axis=-1, keepdims=True)
    ss1 = jnp.sum(jnp.where(low, 0.0, sq), axis=-1, keepdims=True)
    return jnp.where(low, lax.rsqrt(ss0 * (1.0 / HEAD_DIM) + EPS),
                     lax.rsqrt(ss1 * (1.0 / HEAD_DIM) + EPS))


def _inproj_kernel(x_ref, sc_ref, sh_ref, g_ref, w_ref, cos_ref, sin_ref, dq_ref, dk_ref,
                   lng_ref, lnb_ref, ws_ref, gb_ref, bf_ref, qg_ref, kg_ref,
                   pret_ref, yg_ref, qa_ref, ka_ref, vt_ref, carry_ref):
    tm = x_ref.shape[1]
    h = _mod_rms(x_ref[0], g_ref[...], sc_ref[0], sh_ref[0]).astype(BF16)

    @pl.when(pl.program_id(1) == 0)
    def _():
        carry_ref[...] = jnp.zeros_like(carry_ref)

    w_tail = w_ref[0, :, N_IN - LANES:N_IN]
    z = jnp.dot(h, w_tail, preferred_element_type=F32) + bf_ref[...]
    logf = jnp.minimum(z, 0.0) - jnp.log1p(jnp.exp(-jnp.abs(z)))
    tri = (lax.broadcasted_iota(jnp.int32, (CHUNK, CHUNK), 1) <=
           lax.broadcasted_iota(jnp.int32, (CHUNK, CHUNK), 0)).astype(F32)
    carry = carry_ref[0:1, :]
    parts = []
    for c in range(tm // CHUNK):
        cum = jnp.dot(tri, logf[c * CHUNK:(c + 1) * CHUNK], precision=lax.Precision.HIGHEST,
                      preferred_element_type=F32) + carry
        carry = cum[CHUNK - 1:CHUNK, :]
        parts.append(cum)
    carry_ref[...] = jnp.broadcast_to(carry, carry_ref.shape)
    neg_cum = jnp.concatenate(parts, axis=0) * (-LOG2E)
    bias_hi = neg_cum.astype(BF16).astype(F32)
    rest = neg_cum - bias_hi
    bias_mid = rest.astype(BF16).astype(F32)
    bias_lo = (rest - bias_mid).astype(BF16).astype(F32)

    lane = _lane_iota((tm, LANES))
    low = lane < HEAD_DIM
    ones_row = jnp.where(lax.broadcasted_iota(jnp.int32, (VT_ROWS - HEAD_DIM, tm), 0) == 0,
                         1.0, 0.0).astype(BF16)

    def fox_tiles(piece, kind, pair):
        if kind == 2:
            v_t = piece.T.astype(BF16)
        else:
            cols = slice(pair * LANES, (pair + 1) * LANES)
            gain = qg_ref[:, cols] * (HEAD_DIM ** -0.5 * LOG2E) if kind == 0 else kg_ref[:, cols]
            normed = piece * _pair_rms_scale(piece) * gain
        for half in range(2):
            hd = 2 * pair + half
            own = low if half == 0 else jnp.logical_not(low)
            b0 = HEAD_DIM * (1 - half)
            if kind == 0:
                ones = jnp.where((lane >= b0) & (lane < b0 + 3), 1.0, 0.0)
                qa_ref[0, hd] = jnp.where(own, normed, ones).astype(BF16)
            elif kind == 1:
                col = slice(FL_LANE0 + hd, FL_LANE0 + hd + 1)
                bias = jnp.where(lane == b0, bias_hi[:, col],
                                 jnp.where(lane == b0 + 1, bias_mid[:, col],
                                           jnp.where(lane == b0 + 2, bias_lo[:, col], 0.0)))
                ka_ref[0, hd] = jnp.where(own, normed, bias).astype(BF16)
            else:
                vt_ref[0, hd, 0, 0:HEAD_DIM, :] = v_t[half * HEAD_DIM:(half + 1) * HEAD_DIM]
                vt_ref[0, hd, 0, HEAD_DIM:VT_ROWS, :] = ones_row

    def spatial_gate(u, vn):
        lane_c = _lane_iota((CHUNK, LANES))
        keep_a = jnp.where(lane_c < HEAD_DIM, 1.0, 0.0).astype(BF16)
        keep_b = jnp.where(lane_c < HEAD_DIM, 0.0, 1.0).astype(BF16)
        row = lax.broadcasted_iota(jnp.int32, (CHUNK, 2 * CHUNK), 0)
        col = _lane_iota((CHUNK, 2 * CHUNK)) % CHUNK
        vb = vn.astype(BF16)
        for pair in range(N_GM_HEADS // 2):
            w_pair = jnp.where(col <= row, ws_ref[:, pair * 2 * CHUNK:(pair + 1) * 2 * CHUNK],
                               0.0).astype(BF16)
            ls = slice(pair * LANES, (pair + 1) * LANES)
            for c in range(tm // CHUNK):
                rs = slice(c * CHUNK, (c + 1) * CHUNK)
                v_tile = vb[rs, ls]
                stacked = jnp.concatenate([v_tile * keep_a, v_tile * keep_b], axis=0)
                mixed = jnp.dot(w_pair, stacked, preferred_element_type=F32)
                yg_ref[0, rs, ls] = (u[rs, ls] * (mixed + gb_ref[:, ls])).astype(BF16)

    def rotary_decay(piece, table_ref, tile):
        rot = piece * cos_ref[...] + _swap_half_heads(piece) * sin_ref[...]
        table = table_ref[:, tile * LANES:(tile + 1) * LANES]
        return rot * jnp.concatenate([table] * (tm // CHUNK), axis=0)

    for lo in range(0, N_MAIN, MXU_DIM):
        width = min(MXU_DIM, N_MAIN - lo)
        r = jnp.dot(h, w_ref[0, :, lo:lo + width], preferred_element_type=F32)
        if lo == GM_COL0:
            gate_u = _gelu(r)
            continue
        if lo == GM_COL0 + D_GM:
            v = _gelu(r)
            mu = jnp.mean(v, axis=-1, keepdims=True)
            vc = v - mu
            var = jnp.mean(vc * vc, axis=-1, keepdims=True)
            spatial_gate(gate_u, vc * lax.rsqrt(var + EPS) * lng_ref[...] + lnb_ref[...])
            continue
        for off in range(0, width, LANES):
            c0 = lo + off
            piece = r[:, off:off + LANES]
            if c0 < D_RET:
                piece = rotary_decay(piece, dq_ref, c0 // LANES)
            elif c0 < 2 * D_RET:
                piece = rotary_decay(piece, dk_ref, (c0 - D_RET) // LANES)
            elif 3 * D_RET <= c0 < 4 * D_RET:
                piece = piece * jax.nn.sigmoid(piece)
            if c0 < GM_COL0:
                pret_ref[0, :, c0:c0 + LANES] = piece.astype(BF16)
            else:
                kind, pair = divmod((c0 - FOX_COL0) // LANES, N_FOX_HEADS // 2)
                fox_tiles(piece, kind, pair)


def _in_projection(x, sc, sh, g, w_in, layer, cos_t, sin_t, dq, dk, ln_g, ln_b, ws_cat,
                   gate_bias, bf_row, qg, kg, tm):
    B, S, D = x.shape
    assert tm == FOX_TK
    const = lambda shape: pl.BlockSpec(shape, lambda b, i: (0,) * len(shape))
    aug = jax.ShapeDtypeStruct((B, N_FOX_HEADS, S, LANES), BF16)
    aug_spec = pl.BlockSpec((1, N_FOX_HEADS, tm, LANES), lambda b, i: (b, 0, i, 0))
    return pl.pallas_call(
        _inproj_kernel,
        out_shape=(jax.ShapeDtypeStruct((B, S, 4 * D_RET), BF16),
                   jax.ShapeDtypeStruct((B, S, D_GM), BF16),
                   aug, aug,
                   jax.ShapeDtypeStruct((B, N_FOX_HEADS, S // tm, VT_ROWS, tm), BF16)),
        grid=(B, S // tm),
        in_specs=[pl.BlockSpec((1, tm, D), lambda b, i: (b, i, 0)),
                  pl.BlockSpec((1, 1, D), lambda b, i: (b, 0, 0)),
                  pl.BlockSpec((1, 1, D), lambda b, i: (b, 0, 0)),
                  const((1, D)),
                  _layer_resident(w_in, layer),
                  pl.BlockSpec((tm, LANES), lambda b, i: (i, 0)),
                  pl.BlockSpec((tm, LANES), lambda b, i: (i, 0)),
                  const((CHUNK, D_RET)), const((CHUNK, D_RET)),
                  const((1, D_GM)), const((1, D_GM)),
                  const((CHUNK, N_GM_HEADS * CHUNK)), const((CHUNK, D_GM)),
                  const((1, LANES)), const((1, D_FOX)), const((1, D_FOX))],
        out_specs=(pl.BlockSpec((1, tm, 4 * D_RET), lambda b, i: (b, i, 0)),
                   pl.BlockSpec((1, tm, D_GM), lambda b, i: (b, i, 0)),
                   aug_spec, aug_spec,
                   pl.BlockSpec((1, N_FOX_HEADS, 1, VT_ROWS, tm), lambda b, i: (b, 0, i, 0, 0))),
        scratch_shapes=[pltpu.VMEM((SUBLANES, LANES), F32)],
        compiler_params=_cparams("arbitrary", "arbitrary"),
        name="in_proj",
    )(x, sc, sh, g, w_in, cos_t, sin_t, dq, dk, ln_g, ln_b, ws_cat, gate_bias, bf_row, qg, kg)


def _ret_kernel(q_ref, k_ref, v_ref, g_ref, ng_ref, wo_ref, wu_ref, wd_ref,
                o_ref, wo_out, wu_out, wd_out, state_ref):
    wo_out[...] = wo_ref[0].astype(BF16)
    wu_out[...] = wu_ref[0].astype(BF16)
    wd_out[...] = wd_ref[0].astype(BF16)

    tr = q_ref.shape[1]
    n_pairs = N_RET_HEADS // 2
    low = _lane_iota((CHUNK, LANES)) < HEAD_DIM
    keep0 = jnp.where(low, 1.0, 0.0).astype(BF16)
    keep1 = jnp.where(low, 0.0, 1.0).astype(BF16)
    causal = (lax.broadcasted_iota(jnp.int32, (CHUNK, CHUNK), 0) >=
              lax.broadcasted_iota(jnp.int32, (CHUNK, CHUNK), 1))
    blockdiag = (lax.broadcasted_iota(jnp.int32, (LANES, LANES), 0) < HEAD_DIM) == low
    dn_t = (((1,), (1,)), ((), ()))
    dn_ta = (((0,), (0,)), ((), ()))

    @pl.when(pl.program_id(1) == 0)
    def _():
        state_ref[...] = jnp.zeros_like(state_ref)

    n_c = tr // CHUNK
    tiles = [(p, c) for p in range(n_pairs) for c in range(n_c)]
    sl = lambda p, c: (slice(c * CHUNK, (c + 1) * CHUNK), slice(p * LANES, (p + 1) * LANES))
    kvs, scores = {}, {}
    for p, c in tiles:
        rs, cs = sl(p, c)
        q, k, v = q_ref[0, rs, cs], k_ref[0, rs, cs], v_ref[0, rs, cs]
        kvs[p, c] = lax.dot_general(k, v, dn_ta, preferred_element_type=F32)
        for hh, keep in enumerate((keep0, keep1)):
            s = lax.dot_general(q * keep, k, dn_t, preferred_element_type=F32)
            scores[p, c, hh] = jnp.where(causal, s, 0.0).astype(BF16)
    states = {}
    for p in range(n_pairs):
        chunk_decay = jnp.where(low[0:1, :], math.exp(_ret_log_gamma(2 * p) * CHUNK),
                                math.exp(_ret_log_gamma(2 * p + 1) * CHUNK))
        state = state_ref[p]
        for c in range(n_c):
            states[p, c] = state.astype(BF16)
            state = (state + jnp.where(blockdiag, kvs[p, c], 0.0)) * chunk_decay
        state_ref[p] = state
    for p, c in tiles:
        rs, cs = sl(p, c)
        q = q_ref[0, rs, cs]
        rhs = jnp.concatenate([v_ref[0, rs, cs], states[p, c]], axis=0)
        ys = [jnp.dot(jnp.concatenate([scores[p, c, hh], q * keep], axis=1), rhs,
                      preferred_element_type=F32) for hh, keep in enumerate((keep0, keep1))]
        y = jnp.where(low, ys[0], ys[1])
        yn = y * _pair_rms_scale(y) * ng_ref[:, cs]
        o_ref[0, rs, cs] = (yn * g_ref[0, rs, cs].astype(F32)).astype(BF16)


def _retention(p_ret, ng, layer, w_o, w_up, w_down, tr):
    B, S, _ = p_ret.shape
    n_s = S // tr
    steps = B * n_s
    col = lambda j: pl.BlockSpec((1, tr, D_RET), lambda b, i, j=j: (b, i, j))

    def slab(w, n_slabs):
        rows = w.shape[1] // n_slabs
        assert rows * n_slabs == w.shape[1] and rows % 16 == 0 and steps % n_slabs == 0
        rep = steps // n_slabs
        return pl.BlockSpec((1, rows) + w.shape[2:],
                            lambda b, i: (layer, (b * n_s + i) // rep, 0))

    def slab_out(w, n_slabs):
        rows = w.shape[1] // n_slabs
        rep = steps // n_slabs
        return pl.BlockSpec((rows,) + w.shape[2:], lambda b, i: ((b * n_s + i) // rep, 0))

    slabs = ((w_o, steps), (w_up, steps), (w_down, steps // 2))
    return pl.pallas_call(
        _ret_kernel,
        out_shape=(jax.ShapeDtypeStruct((B, S, D_RET), BF16),)
        + tuple(jax.ShapeDtypeStruct(w.shape[1:], BF16) for w, _ in slabs),
        grid=(B, n_s),
        in_specs=[col(0), col(1), col(2), col(3),
                  pl.BlockSpec((1, D_RET), lambda b, i: (0, 0))]
        + [slab(w, n) for w, n in slabs],
        out_specs=(pl.BlockSpec((1, tr, D_RET), lambda b, i: (b, i, 0)),)
        + tuple(slab_out(w, n) for w, n in slabs),
        scratch_shapes=[pltpu.VMEM((N_RET_HEADS // 2, LANES, LANES), F32)],
        compiler_params=_cparams("arbitrary", "arbitrary"),
        name="retention",
    )(p_ret, p_ret, p_ret, p_ret, ng, w_o, w_up, w_down)


def _fox_kernel(q_ref, k_ref, vt_ref, o_ref, m_ref, acc_ref, s0_ref, s1_ref, mb0_ref, mb1_ref):
    s_refs = (s0_ref, s1_ref)
    mb_refs = (mb0_ref, mb1_ref)
    i = pl.program_id(2)
    n_qt = FOX_BQ // FOX_TQ
    units = [(hh, qt) for hh in range(2) for qt in range(n_qt)]
    n_u = len(units)
    dn_t = (((1,), (1,)), ((), ()))
    m_ref[...] = jnp.full(m_ref.shape, NEG_INF, F32)
    acc_ref[...] = jnp.zeros_like(acc_ref)

    def diag_keys(qt, half):
        d = qt * FOX_TQ - half * FOX_TK
        nk = min(max(d + FOX_TQ, 0), FOX_TK)
        return nk, (d if nk - 1 > d else None)

    def scores(j, slot, u):
        hh, qt = units[u]
        start = pl.multiple_of(j * FOX_TK, FOX_TK)
        k = k_ref[0, hh, pl.ds(start, FOX_TK), :]
        q = q_ref[0, hh, qt * FOX_TQ:(qt + 1) * FOX_TQ, :]
        s = lax.dot_general(k, q, dn_t, preferred_element_type=F32)
        s_refs[slot][u] = s
        mb_refs[slot][u] = jnp.max(s, axis=0, keepdims=True)

    def update(j, slot, u, diag_half=None):
        hh, qt = units[u]
        nk, mask_off = (FOX_TK, None) if diag_half is None else diag_keys(qt, diag_half)
        if nk == 0:
            return
        s = s_refs[slot][u, 0:nk, :]
        m_blk = mb_refs[slot][u]
        if mask_off is not None or nk < FOX_TK:
            if mask_off is not None:
                key = lax.broadcasted_iota(jnp.int32, s.shape, 0)
                qry = lax.broadcasted_iota(jnp.int32, s.shape, 1)
                s = jnp.where(key <= qry + mask_off, s, NEG_INF)
            m_blk = jnp.max(s, axis=0, keepdims=True)
        m_old = m_ref[hh, qt]
        m_new = jnp.maximum(m_old, m_blk)
        p = jnp.exp2((s - m_new).astype(BF16))
        vt = vt_ref[0, hh, j, :, 0:nk]
        acc_ref[hh, qt] = jnp.exp2(m_old - m_new) * acc_ref[hh, qt] + jnp.dot(
            vt, p, preferred_element_type=F32)
        m_ref[hh, qt] = m_new

    for u in range(n_u):
        scores(0, 0, u)

    def body(j, carry):
        for u in range(n_u):
            scores(2 * j + 1, 1, u)
            update(2 * j, 0, u)
        for u in range(n_u):
            scores(2 * j + 2, 0, u)
            update(2 * j + 1, 1, u)
        return carry

    lax.fori_loop(0, i, body, 0)
    for u in range(n_u):
        if diag_keys(units[u][1], 1)[0] > 0:
            scores(2 * i + 1, 1, u)
        update(2 * i, 0, u, 0)
    for u in range(n_u):
        update(2 * i + 1, 1, u, 1)

    for qt in range(n_qt):
        rows = []
        for hh in range(2):
            acc = acc_ref[hh, qt]
            rows.append(acc[0:HEAD_DIM] * (1.0 / acc[HEAD_DIM:HEAD_DIM + 1]))
        pair_t = jnp.concatenate(rows, axis=0)
        o_ref[0, qt * FOX_TQ:(qt + 1) * FOX_TQ, :] = pair_t.T.astype(BF16)


def _fox_attention(qa, ka, vt):
    B, H, S, _ = qa.shape
    n_qt = FOX_BQ // FOX_TQ
    stage = pltpu.VMEM((2 * n_qt, FOX_TK, FOX_TQ), F32)
    stage_max = pltpu.VMEM((2 * n_qt, 1, FOX_TQ), F32)
    return pl.pallas_call(
        _fox_kernel,
        out_shape=jax.ShapeDtypeStruct((B, S, D_FOX), BF16),
        grid=(B, H // 2, S // FOX_BQ),
        in_specs=[pl.BlockSpec((1, 2, FOX_BQ, LANES), lambda b, p, i: (b, p, i, 0)),
                  pl.BlockSpec((1, 2, S, LANES), lambda b, p, i: (b, p, 0, 0)),
                  pl.BlockSpec((1, 2, S // FOX_TK, VT_ROWS, FOX_TK),
                               lambda b, p, i: (b, p, 0, 0, 0))],
        out_specs=pl.BlockSpec((1, FOX_BQ, LANES), lambda b, p, i: (b, i, p)),
        scratch_shapes=[pltpu.VMEM((2, n_qt, 1, FOX_TQ), F32),
                        pltpu.VMEM((2, n_qt, VT_ROWS, FOX_TQ), F32),
                        stage, stage, stage_max, stage_max],
        compiler_params=_cparams("arbitrary", "arbitrary", "arbitrary"),
        name="fox_attention",
    )(qa, ka, vt)


FF_CHUNK = MXU_DIM


def _mix_ffn_kernel(yr_ref, yg_ref, yf_ref, x_ref, gt1_ref, sc_ref, sh_ref, gt2_ref, g_ref,
                    wo_ref, wu_ref, cw_ref, cb_ref, wd_ref, o_ref, carry_ref, stage_ref, act_ref):
    tm = x_ref.shape[1]

    @pl.when(pl.program_id(1) == 0)
    def _():
        carry_ref[...] = jnp.zeros_like(carry_ref)

    halves = []
    for rs in (slice(0, tm // 2), slice(tm // 2, tm)):
        mix = jnp.concatenate([yr_ref[0, rs], yg_ref[0, rs], yf_ref[0, rs]], axis=-1)
        xn = x_ref[0, rs] + gt1_ref[0] * jnp.dot(mix, wo_ref[...], preferred_element_type=F32)
        o_ref[0, rs] = xn
        halves.append(xn)
    h = jnp.concatenate([_mod_rms(xn, g_ref[...], sc_ref[0], sh_ref[0]).astype(BF16)
                         for xn in halves], axis=0)

    def conv_cols(c0):
        cs = slice(c0, c0 + FF_CHUNK)
        up = jnp.dot(h, wu_ref[:, cs], preferred_element_type=F32)
        stage_ref[0:SUBLANES, :] = carry_ref[:, cs]
        stage_ref[SUBLANES:, :] = up
        carry_ref[:, cs] = up[tm - SUBLANES:, :]
        prev1 = stage_ref[SUBLANES - 1:SUBLANES - 1 + tm, :]
        prev2 = stage_ref[SUBLANES - 2:SUBLANES - 2 + tm, :]
        return (cw_ref[0, 2:3, cs] * up + cw_ref[0, 1:2, cs] * prev1 + cw_ref[0, 0:1, cs] * prev2
                + cb_ref[0, :, cs])

    for f in range(D_FF // FF_CHUNK):
        a = conv_cols(f * FF_CHUNK)
        b = conv_cols(D_FF + f * FF_CHUNK)
        act_ref[:, f * FF_CHUNK:(f + 1) * FF_CHUNK] = ((a * jax.nn.sigmoid(a)) * b).astype(BF16)
    down = jnp.dot(act_ref[...], wd_ref[...], preferred_element_type=F32)
    o_ref[0] = o_ref[0] + gt2_ref[0] * down


def _mix_ffn(yr, yg, yf, x, gt1, sc, sh, gt2, g, layer, w_o, w_up, conv_w, conv_b, w_down, tm):
    B, S, D = x.shape
    row = lambda n: pl.BlockSpec((1, tm, n), lambda b, i: (b, i, 0))
    vec = pl.BlockSpec((1, 1, D), lambda b, i: (b, 0, 0))
    return pl.pallas_call(
        _mix_ffn_kernel,
        out_shape=jax.ShapeDtypeStruct((B, S, D), F32),
        grid=(B, S // tm),
        in_specs=[row(D_RET), row(D_GM), row(D_FOX), row(D), vec, vec, vec, vec,
                  pl.BlockSpec((1, D), lambda b, i: (0, 0)),
                  _resident(w_o), _resident(w_up),
                  _layer_resident(conv_w, layer), _layer_resident(conv_b, layer),
                  _resident(w_down)],
        out_specs=row(D),
        scratch_shapes=[pltpu.VMEM((SUBLANES, 2 * D_FF), F32),
                        pltpu.VMEM((tm + SUBLANES, FF_CHUNK), F32),
                        pltpu.VMEM((tm, D_FF), BF16)],
        compiler_params=_cparams("arbitrary", "arbitrary"),
        name="mix_ffn",
    )(yr, yg, yf, x, gt1, sc, sh, gt2, g, w_o, w_up, conv_w, conv_b, w_down)


def kernel(x, c, ada_w, ada_b, norm1_g, w_in, ret_norm_g, gm_ln_g, gm_ln_b, gm_ws, gm_bs,
           fox_qn_g, fox_kn_g, fox_bf, w_o, norm2_g, w_up, conv_w, conv_b, w_down):
    B, S, D = x.shape
    L = ada_w.shape[0]
    assert D == D_MODEL and S % FOX_BQ == 0
    tm = 512

    mod = _modulation(c, ada_w, ada_b).reshape(L, B, 6, 1, D)
    cos_t, sin_t, dq, dk = _rope_tables(S)
    w_in_b = w_in.astype(BF16)
    conv_b3 = conv_b.reshape(L, 1, 2 * D_FF)

    for l in range(L):
        sh1, sc1, gt1, sh2, sc2, gt2 = (mod[l, :, i] for i in range(6))
        ws_cat = gm_ws[l].transpose(1, 0, 2).reshape(CHUNK, N_GM_HEADS * CHUNK)
        gate_bias = jnp.repeat(gm_bs[l].T, HEAD_DIM, axis=1)
        bf_row = jnp.pad(fox_bf[l], (FL_LANE0, 0)).reshape(1, LANES)
        p_ret, y_g, qa, ka, vt = _in_projection(
            x, sc1, sh1, norm1_g[l].reshape(1, D), w_in_b, l, cos_t, sin_t, dq, dk,
            gm_ln_g[l].reshape(1, D_GM), gm_ln_b[l].reshape(1, D_GM), ws_cat, gate_bias, bf_row,
            jnp.tile(fox_qn_g[l], N_FOX_HEADS).reshape(1, D_FOX),
            jnp.tile(fox_kn_g[l], N_FOX_HEADS).reshape(1, D_FOX), tm)

        y_r, w_o_b, w_up_b, w_down_b = _retention(p_ret, ret_norm_g[l].reshape(1, D_RET), l,
                                                   w_o, w_up, w_down, tm)
        y_f = _fox_attention(qa, ka, vt)

        x = _mix_ffn(y_r, y_g, y_f, x, gt1, sc2, sh2, gt2, norm2_g[l].reshape(1, D), l,
                     w_o_b, w_up_b, conv_w, conv_b3, w_down_b, tm)
    return x
```

```python
import functools
import math

import jax
import jax.numpy as jnp
from jax import lax
from jax.experimental import pallas as pl
from jax.experimental.pallas import tpu as pltpu

D_MODEL = 1024
HEAD_DIM = 64
N_RET_HEADS = 6
N_GM_HEADS = 4
N_FOX_HEADS = 6
D_RET = N_RET_HEADS * HEAD_DIM
D_GM = N_GM_HEADS * HEAD_DIM
D_FOX = N_FOX_HEADS * HEAD_DIM
CHUNK = 128
D_FF = 2816
ROPE_BASE = 10000.0
EPS = 1e-6
NEG_INF = -1e30

LANES = 128
SUBLANES = 8
MXU_DIM = 256
VMEM_LIMIT = 52 * 1024 * 1024

N_MAIN = 4 * D_RET + 2 * D_GM + 3 * D_FOX
GM_COL0 = 4 * D_RET
FOX_COL0 = GM_COL0 + 2 * D_GM
N_IN = N_MAIN + N_FOX_HEADS
FL_LANE0 = LANES - N_FOX_HEADS

LOG2E = math.log2(math.e)
FOX_TK = 1024
FOX_TQ = MXU_DIM
FOX_BQ = 2 * FOX_TK
VT_ROWS = 80

F32 = jnp.float32
BF16 = jnp.bfloat16


def _cparams(*sem):
    return pltpu.CompilerParams(dimension_semantics=sem, vmem_limit_bytes=VMEM_LIMIT)


def _resident(whole):
    nd = whole.ndim
    return pl.BlockSpec(whole.shape, lambda *_: (0,) * nd, pipeline_mode=pl.Buffered(1))


def _layer_resident(stacked, layer):
    nd = stacked.ndim - 1
    return pl.BlockSpec((1,) + stacked.shape[1:], lambda *_: (layer,) + (0,) * nd,
                        pipeline_mode=pl.Buffered(1))


def _lane_iota(shape):
    return lax.broadcasted_iota(jnp.int32, shape, len(shape) - 1)


def _mod_kernel(ct_ref, w_ref, b_ref, o_ref):
    ct = ct_ref[...]
    cond_t = ct * jax.nn.sigmoid(ct)
    w = w_ref[0]
    rows = [jnp.sum(cond_t[:, b:b + 1] * w, axis=0, keepdims=True)
            for b in range(ct.shape[1])]
    o_ref[0] = jnp.concatenate(rows, axis=0) + b_ref[0]


def _modulation(c, ada_w, ada_b):
    L, D, N = ada_w.shape
    B = c.shape[0]
    tn = 1536
    return pl.pallas_call(
        _mod_kernel,
        out_shape=jax.ShapeDtypeStruct((L, B, N), F32),
        grid=(L, N // tn),
        in_specs=[pl.BlockSpec((D, B), lambda l, j: (0, 0)),
                  pl.BlockSpec((1, D, tn), lambda l, j: (l, 0, j)),
                  pl.BlockSpec((1, 1, tn), lambda l, j: (l, 0, j))],
        out_specs=pl.BlockSpec((1, B, tn), lambda l, j: (l, 0, j)),
        compiler_params=_cparams("arbitrary", "arbitrary"),
        name="adaln_mod",
    )(c.T, ada_w, ada_b.reshape(L, 1, N))


def _ret_log_gamma(h):
    return math.log(1.0 - 2.0 ** (-5.0 - h))


def _rope_kernel(inv_ref, cos_ref, sin_ref, dq_ref, dk_ref):
    ts = cos_ref.shape[0]
    pos = (lax.broadcasted_iota(jnp.int32, (ts, LANES), 0) + pl.program_id(0) * ts).astype(F32)
    ang = pos * inv_ref[...]
    first_half = (_lane_iota((ts, LANES)) % HEAD_DIM) < (HEAD_DIM // 2)
    cos_ref[...] = jnp.cos(ang)
    s = jnp.sin(ang)
    sin_ref[...] = jnp.where(first_half, -s, s)

    head = _lane_iota((CHUNK, D_RET)) // HEAD_DIM
    lg = jnp.zeros((CHUNK, D_RET), F32)
    for h in range(N_RET_HEADS):
        lg = jnp.where(head == h, _ret_log_gamma(h), lg)
    t1 = lax.broadcasted_iota(jnp.int32, (CHUNK, D_RET), 0).astype(F32) + 1.0
    dq_ref[...] = jnp.exp(lg * t1)
    dk_ref[...] = jnp.exp(-lg * t1) * (HEAD_DIM ** -0.5)


def _rope_tables(S):
    half = HEAD_DIM // 2
    inv = ROPE_BASE ** (-jnp.arange(half, dtype=F32) / half)
    inv_row = jnp.tile(inv, LANES // half).reshape(1, LANES)
    ts = 512
    decay = jax.ShapeDtypeStruct((CHUNK, D_RET), F32)
    decay_spec = pl.BlockSpec((CHUNK, D_RET), lambda i: (0, 0))
    return pl.pallas_call(
        _rope_kernel,
        out_shape=(jax.ShapeDtypeStruct((S, LANES), F32),) * 2 + (decay, decay),
        grid=(S // ts,),
        in_specs=[pl.BlockSpec((1, LANES), lambda i: (0, 0))],
        out_specs=(pl.BlockSpec((ts, LANES), lambda i: (i, 0)),) * 2 + (decay_spec, decay_spec),
        compiler_params=_cparams("arbitrary"),
        name="rope_tables",
    )(inv_row)


def _mod_rms(x, g, sc, sh):
    ms = jnp.mean(x * x, axis=-1, keepdims=True)
    return (x * lax.rsqrt(ms + EPS) * g) * (1.0 + sc) + sh


def _swap_half_heads(x):
    first_half = (_lane_iota(x.shape) % HEAD_DIM) < (HEAD_DIM // 2)
    return jnp.where(first_half, pltpu.roll(x, LANES - HEAD_DIM // 2, 1),
                     pltpu.roll(x, HEAD_DIM // 2, 1))


def _gelu(x):
    return 0.5 * x * (1.0 + jnp.tanh(math.sqrt(2.0 / math.pi) * (x + 0.044715 * (x * x * x))))


def _pair_rms_scale(x):
    low = _lane_iota(x.shape) < HEAD_DIM
    sq = x * x
    ss0 = jnp.sum(jnp.where(low, sq, 0.0), axis=-1, keepdims=True)
    ss1 = jnp.sum(jnp.where(low, 0.0, sq), axis=-1, keepdims=True)
    return jnp.where(low, lax.rsqrt(ss0 * (1.0 / HEAD_DIM) + EPS),
                     lax.rsqrt(ss1 * (1.0 / HEAD_DIM) + EPS))


def _inproj_kernel(x_ref, xn_ref, sc_ref, sh_ref, g_ref, w_ref, cos_ref, sin_ref, dq_ref, dk_ref,
                   lng_ref, lnb_ref, ws_ref, gb_ref, bf_ref, qg_ref, kg_ref,
                   pret_ref, yg_ref, qa_ref, ka_ref, vt_ref,
                   h_a, h_b, nc_a, nc_b, carry_ref):
    tm = x_ref.shape[1]
    step = pl.program_id(1)

    def prepare(x_tile_ref, h_out, nc_out):
        stages, box = [], {}

        def norm_rows(q):
            rs = slice(q * CHUNK, (q + 1) * CHUNK)
            h_out[rs, :] = _mod_rms(x_tile_ref[0, rs], g_ref[...], sc_ref[0], sh_ref[0]).astype(BF16)

        def forget_logits():
            z = jnp.dot(h_out[...], w_ref[0, :, N_IN - LANES:N_IN],
                        preferred_element_type=F32) + bf_ref[...]
            box["logf"] = jnp.minimum(z, 0.0) - jnp.log1p(jnp.exp(-jnp.abs(z)))

        def cumulate():
            tri = (lax.broadcasted_iota(jnp.int32, (CHUNK, CHUNK), 1) <=
                   lax.broadcasted_iota(jnp.int32, (CHUNK, CHUNK), 0)).astype(F32)
            logf = box["logf"]
            within = [jnp.dot(tri, logf[c * CHUNK:(c + 1) * CHUNK],
                              precision=lax.Precision.HIGHEST, preferred_element_type=F32)
                      for c in range(tm // CHUNK)]
            carry = carry_ref[0:1, :]
            for c, cum in enumerate(within):
                nc_out[c * CHUNK:(c + 1) * CHUNK, :] = (cum + carry) * (-LOG2E)
                carry = carry + cum[CHUNK - 1:CHUNK, :]
            carry_ref[...] = jnp.broadcast_to(carry, carry_ref.shape)

        for q in range(tm // CHUNK):
            stages.append(functools.partial(norm_rows, q))
        return stages + [forget_logits, cumulate]

    @pl.when(step == 0)
    def _():
        carry_ref[...] = jnp.zeros_like(carry_ref)
        for stage in prepare(x_ref, h_a, nc_a):
            stage()

    @pl.when(step % 2 == 0)
    def _():
        _inproj_tile(h_a, nc_a, prepare(xn_ref, h_b, nc_b), w_ref, cos_ref, sin_ref,
                     dq_ref, dk_ref, lng_ref, lnb_ref, ws_ref, gb_ref, qg_ref, kg_ref,
                     pret_ref, yg_ref, qa_ref, ka_ref, vt_ref)

    @pl.when(step % 2 == 1)
    def _():
        _inproj_tile(h_b, nc_b, prepare(xn_ref, h_a, nc_a), w_ref, cos_ref, sin_ref,
                     dq_ref, dk_ref, lng_ref, lnb_ref, ws_ref, gb_ref, qg_ref, kg_ref,
                     pret_ref, yg_ref, qa_ref, ka_ref, vt_ref)


def _inproj_tile(h_ref, nc_ref, next_stages, w_ref, cos_ref, sin_ref, dq_ref, dk_ref, lng_ref,
                 lnb_ref, ws_ref, gb_ref, qg_ref, kg_ref, pret_ref, yg_ref, qa_ref, ka_ref, vt_ref):
    tm = h_ref.shape[0]
    h = h_ref[...]
    neg_cum = nc_ref[...]
    bias_hi = neg_cum.astype(BF16).astype(F32)
    rest = neg_cum - bias_hi
    bias_mid = rest.astype(BF16).astype(F32)
    bias_lo = (rest - bias_mid).astype(BF16).astype(F32)

    lane = _lane_iota((tm, LANES))
    low = lane < HEAD_DIM
    ones_row = jnp.where(lax.broadcasted_iota(jnp.int32, (VT_ROWS - HEAD_DIM, tm), 0) == 0,
                         1.0, 0.0).astype(BF16)

    def fox_tiles(piece, kind, pair):
        if kind == 2:
            v_t = piece.T.astype(BF16)
        else:
            cols = slice(pair * LANES, (pair + 1) * LANES)
            gain = qg_ref[:, cols] * (HEAD_DIM ** -0.5 * LOG2E) if kind == 0 else kg_ref[:, cols]
            normed = piece * _pair_rms_scale(piece) * gain
        for half in range(2):
            hd = 2 * pair + half
            own = low if half == 0 else jnp.logical_not(low)
            b0 = HEAD_DIM * (1 - half)
            if kind == 0:
                ones = jnp.where((lane >= b0) & (lane < b0 + 3), 1.0, 0.0)
                qa_ref[0, hd] = jnp.where(own, normed, ones).astype(BF16)
            elif kind == 1:
                col = slice(FL_LANE0 + hd, FL_LANE0 + hd + 1)
                bias = jnp.where(lane == b0, bias_hi[:, col],
                                 jnp.where(lane == b0 + 1, bias_mid[:, col],
                                           jnp.where(lane == b0 + 2, bias_lo[:, col], 0.0)))
                ka_ref[0, hd] = jnp.where(own, normed, bias).astype(BF16)
            else:
                vt_ref[0, hd, 0, 0:HEAD_DIM, :] = v_t[half * HEAD_DIM:(half + 1) * HEAD_DIM]
                vt_ref[0, hd, 0, HEAD_DIM:VT_ROWS, :] = ones_row

    def spatial_gate(u, vn):
        lane_c = _lane_iota((CHUNK, LANES))
        keep_a = jnp.where(lane_c < HEAD_DIM, 1.0, 0.0).astype(BF16)
        keep_b = jnp.where(lane_c < HEAD_DIM, 0.0, 1.0).astype(BF16)
        row = lax.broadcasted_iota(jnp.int32, (CHUNK, 2 * CHUNK), 0)
        col = _lane_iota((CHUNK, 2 * CHUNK)) % CHUNK
        vb = vn.astype(BF16)
        for pair in range(N_GM_HEADS // 2):
            w_pair = jnp.where(col <= row, ws_ref[:, pair * 2 * CHUNK:(pair + 1) * 2 * CHUNK],
                               0.0).astype(BF16)
            ls = slice(pair * LANES, (pair + 1) * LANES)
            for c in range(tm // CHUNK):
                rs = slice(c * CHUNK, (c + 1) * CHUNK)
                v_tile = vb[rs, ls]
                stacked = jnp.concatenate([v_tile * keep_a, v_tile * keep_b], axis=0)
                mixed = jnp.dot(w_pair, stacked, preferred_element_type=F32)
                yg_ref[0, rs, ls] = (u[rs, ls] * (mixed + gb_ref[:, ls])).astype(BF16)

    def rotary_decay(piece, table_ref, tile):
        rot = piece * cos_ref[...] + _swap_half_heads(piece) * sin_ref[...]
        table = table_ref[:, tile * LANES:(tile + 1) * LANES]
        return rot * jnp.concatenate([table] * (tm // CHUNK), axis=0)

    for lo in range(0, N_MAIN, MXU_DIM):
        width = min(MXU_DIM, N_MAIN - lo)
        r = jnp.dot(h, w_ref[0, :, lo:lo + width], preferred_element_type=F32)
        if next_stages:
            next_stages.pop(0)()
        if lo == GM_COL0:
            gate_u = _gelu(r)
            continue
        if lo == GM_COL0 + D_GM:
            v = _gelu(r)
            mu = jnp.mean(v, axis=-1, keepdims=True)
            vc = v - mu
            var = jnp.mean(vc * vc, axis=-1, keepdims=True)
            spatial_gate(gate_u, vc * lax.rsqrt(var + EPS) * lng_ref[...] + lnb_ref[...])
            continue
        for off in range(0, width, LANES):
            c0 = lo + off
            piece = r[:, off:off + LANES]
            if c0 < D_RET:
                piece = rotary_decay(piece, dq_ref, c0 // LANES)
            elif c0 < 2 * D_RET:
                piece = rotary_decay(piece, dk_ref, (c0 - D_RET) // LANES)
            elif 3 * D_RET <= c0 < 4 * D_RET:
                piece = piece * jax.nn.sigmoid(piece)
            if c0 < GM_COL0:
                pret_ref[0, :, c0:c0 + LANES] = piece.astype(BF16)
            else:
                kind, pair = divmod((c0 - FOX_COL0) // LANES, N_FOX_HEADS // 2)
                fox_tiles(piece, kind, pair)


def _in_projection(x, sc, sh, g, w_in, layer, cos_t, sin_t, dq, dk, ln_g, ln_b, ws_cat,
                   gate_bias, bf_row, qg, kg, tm):
    B, S, D = x.shape
    assert FOX_TK % tm == 0 and (S // tm) % 2 == 0
    per_kb = FOX_TK // tm
    n_s = S // tm
    const = lambda shape: pl.BlockSpec(shape, lambda b, i: (0,) * len(shape))
    aug = jax.ShapeDtypeStruct((B, N_FOX_HEADS, S, LANES), BF16)
    aug_spec = pl.BlockSpec((1, N_FOX_HEADS, tm, LANES), lambda b, i: (b, 0, i, 0))
    return pl.pallas_call(
        _inproj_kernel,
        out_shape=(jax.ShapeDtypeStruct((B, S, 4 * D_RET), BF16),
                   jax.ShapeDtypeStruct((B, S, D_GM), BF16),
                   aug, aug,
                   jax.ShapeDtypeStruct((B, N_FOX_HEADS, S // FOX_TK, VT_ROWS, FOX_TK), BF16)),
        grid=(B, S // tm),
        in_specs=[pl.BlockSpec((1, tm, D), lambda b, i: (b, i, 0)),
                  pl.BlockSpec((1, tm, D), lambda b, i: (b, jnp.minimum(i + 1, n_s - 1), 0)),
                  pl.BlockSpec((1, 1, D), lambda b, i: (b, 0, 0)),
                  pl.BlockSpec((1, 1, D), lambda b, i: (b, 0, 0)),
                  const((1, D)),
                  _layer_resident(w_in, layer),
                  pl.BlockSpec((tm, LANES), lambda b, i: (i, 0)),
                  pl.BlockSpec((tm, LANES), lambda b, i: (i, 0)),
                  const((CHUNK, D_RET)), const((CHUNK, D_RET)),
                  const((1, D_GM)), const((1, D_GM)),
                  const((CHUNK, N_GM_HEADS * CHUNK)), const((CHUNK, D_GM)),
                  const((1, LANES)), const((1, D_FOX)), const((1, D_FOX))],
        out_specs=(pl.BlockSpec((1, tm, 4 * D_RET), lambda b, i: (b, i, 0)),
                   pl.BlockSpec((1, tm, D_GM), lambda b, i: (b, i, 0)),
                   aug_spec, aug_spec,
                   pl.BlockSpec((1, N_FOX_HEADS, 1, VT_ROWS, tm),
                                lambda b, i: (b, 0, i // per_kb, 0, i % per_kb))),
        scratch_shapes=[pltpu.VMEM((tm, D), BF16), pltpu.VMEM((tm, D), BF16),
                        pltpu.VMEM((tm, LANES), F32), pltpu.VMEM((tm, LANES), F32),
                        pltpu.VMEM((SUBLANES, LANES), F32)],
        compiler_params=_cparams("arbitrary", "arbitrary"),
        name="in_proj",
    )(x, x, sc, sh, g, w_in, cos_t, sin_t, dq, dk, ln_g, ln_b, ws_cat, gate_bias, bf_row, qg, kg)


def _ret_kernel(q_ref, k_ref, v_ref, g_ref, ng_ref, wo_ref, wu_ref, wd_ref,
                o_ref, wo_out, wu_out, wd_out, state_ref):
    wo_out[...] = wo_ref[0].astype(BF16)
    wu_out[...] = wu_ref[0].astype(BF16)
    wd_out[...] = wd_ref[0].astype(BF16)

    tr = q_ref.shape[1]
    n_pairs = N_RET_HEADS // 2
    low = _lane_iota((CHUNK, LANES)) < HEAD_DIM
    keep0 = jnp.where(low, 1.0, 0.0).astype(BF16)
    keep1 = jnp.where(low, 0.0, 1.0).astype(BF16)
    causal = (lax.broadcasted_iota(jnp.int32, (CHUNK, CHUNK), 0) >=
              lax.broadcasted_iota(jnp.int32, (CHUNK, CHUNK), 1))
    blockdiag = (lax.broadcasted_iota(jnp.int32, (LANES, LANES), 0) < HEAD_DIM) == low
    dn_t = (((1,), (1,)), ((), ()))
    dn_ta = (((0,), (0,)), ((), ()))

    @pl.when(pl.program_id(1) == 0)
    def _():
        state_ref[...] = jnp.zeros_like(state_ref)

    n_c = tr // CHUNK
    tiles = [(p, c) for p in range(n_pairs) for c in range(n_c)]
    sl = lambda p, c: (slice(c * CHUNK, (c + 1) * CHUNK), slice(p * LANES, (p + 1) * LANES))
    kvs, scores = {}, {}
    for p, c in tiles:
        rs, cs = sl(p, c)
        q, k, v = q_ref[0, rs, cs], k_ref[0, rs, cs], v_ref[0, rs, cs]
        kvs[p, c] = lax.dot_general(k, v, dn_ta, preferred_element_type=F32)
        for hh, keep in enumerate((keep0, keep1)):
            s = lax.dot_general(q * keep, k, dn_t, preferred_element_type=F32)
            scores[p, c, hh] = jnp.where(causal, s, 0.0).astype(BF16)
    states = {}
    for p in range(n_pairs):
        chunk_decay = jnp.where(low[0:1, :], math.exp(_ret_log_gamma(2 * p) * CHUNK),
                                math.exp(_ret_log_gamma(2 * p + 1) * CHUNK))
        state = state_ref[p]
        for c in range(n_c):
            states[p, c] = state.astype(BF16)
            state = (state + jnp.where(blockdiag, kvs[p, c], 0.0)) * chunk_decay
        state_ref[p] = state
    for p, c in tiles:
        rs, cs = sl(p, c)
        q = q_ref[0, rs, cs]
        rhs = jnp.concatenate([v_ref[0, rs, cs], states[p, c]], axis=0)
        ys = [jnp.dot(jnp.concatenate([scores[p, c, hh], q * keep], axis=1), rhs,
                      preferred_element_type=F32) for hh, keep in enumerate((keep0, keep1))]
        y = jnp.where(low, ys[0], ys[1])
        yn = y * _pair_rms_scale(y) * ng_ref[:, cs]
        o_ref[0, rs, cs] = (yn * g_ref[0, rs, cs].astype(F32)).astype(BF16)


def _retention(p_ret, ng, layer, w_o, w_up, w_down, tr):
    B, S, _ = p_ret.shape
    n_s = S // tr
    steps = B * n_s
    col = lambda j: pl.BlockSpec((1, tr, D_RET), lambda b, i, j=j: (b, i, j))

    def slab(w, n_slabs):
        rows = w.shape[1] // n_slabs
        assert rows * n_slabs == w.shape[1] and rows % 16 == 0 and steps % n_slabs == 0
        rep = steps // n_slabs
        return pl.BlockSpec((1, rows) + w.shape[2:],
                            lambda b, i: (layer, (b * n_s + i) // rep, 0))

    def slab_out(w, n_slabs):
        rows = w.shape[1] // n_slabs
        rep = steps // n_slabs
        return pl.BlockSpec((rows,) + w.shape[2:], lambda b, i: ((b * n_s + i) // rep, 0))

    slabs = ((w_o, steps), (w_up, steps), (w_down, steps // 2))
    return pl.pallas_call(
        _ret_kernel,
        out_shape=(jax.ShapeDtypeStruct((B, S, D_RET), BF16),)
        + tuple(jax.ShapeDtypeStruct(w.shape[1:], BF16) for w, _ in slabs),
        grid=(B, n_s),
        in_specs=[col(0), col(1), col(2), col(3),
                  pl.BlockSpec((1, D_RET), lambda b, i: (0, 0))]
        + [slab(w, n) for w, n in slabs],
        out_specs=(pl.BlockSpec((1, tr, D_RET), lambda b, i: (b, i, 0)),)
        + tuple(slab_out(w, n) for w, n in slabs),
        scratch_shapes=[pltpu.VMEM((N_RET_HEADS // 2, LANES, LANES), F32)],
        compiler_params=_cparams("arbitrary", "arbitrary"),
        name="retention",
    )(p_ret, p_ret, p_ret, p_ret, ng, w_o, w_up, w_down)


def _fox_kernel(q_ref, k_ref, vt_ref, o_ref, m_ref, acc_ref, s0_ref, s1_ref, mb0_ref, mb1_ref):
    s_refs = (s0_ref, s1_ref)
    mb_refs = (mb0_ref, mb1_ref)
    i = pl.program_id(2)
    n_qt = FOX_BQ // FOX_TQ
    units = [(hh, qt) for hh in range(2) for qt in range(n_qt)]
    n_u = len(units)
    dn_t = (((1,), (1,)), ((), ()))
    m_ref[...] = jnp.full(m_ref.shape, NEG_INF, F32)
    acc_ref[...] = jnp.zeros_like(acc_ref)

    def diag_keys(qt, half):
        d = qt * FOX_TQ - half * FOX_TK
        nk = min(max(d + FOX_TQ, 0), FOX_TK)
        return nk, (d if nk - 1 > d else None)

    def scores(j, slot, u):
        hh, qt = units[u]
        start = pl.multiple_of(j * FOX_TK, FOX_TK)
        k = k_ref[0, hh, pl.ds(start, FOX_TK), :]
        q = q_ref[0, hh, qt * FOX_TQ:(qt + 1) * FOX_TQ, :]
        s = lax.dot_general(k, q, dn_t, preferred_element_type=F32)
        s_refs[slot][u] = s
        mb_refs[slot][u] = jnp.max(s, axis=0, keepdims=True)

    def update(j, slot, u, diag_half=None):
        hh, qt = units[u]
        nk, mask_off = (FOX_TK, None) if diag_half is None else diag_keys(qt, diag_half)
        if nk == 0:
            return
        s = s_refs[slot][u, 0:nk, :]
        m_blk = mb_refs[slot][u]
        if mask_off is not None or nk < FOX_TK:
            if mask_off is not None:
                key = lax.broadcasted_iota(jnp.int32, s.shape, 0)
                qry = lax.broadcasted_iota(jnp.int32, s.shape, 1)
                s = jnp.where(key <= qry + mask_off, s, NEG_INF)
            m_blk = jnp.max(s, axis=0, keepdims=True)
        m_old = m_ref[hh, qt]
        m_new = jnp.maximum(m_old, m_blk)
        p = jnp.exp2((s - m_new).astype(BF16))
        vt = vt_ref[0, hh, j, :, 0:nk]
        acc_ref[hh, qt] = jnp.exp2(m_old - m_new) * acc_ref[hh, qt] + jnp.dot(
            vt, p, preferred_element_type=F32)
        m_ref[hh, qt] = m_new

    for u in range(n_u):
        scores(0, 0, u)

    def body(j, carry):
        for u in range(n_u):
            scores(2 * j + 1, 1, u)
            update(2 * j, 0, u)
        for u in range(n_u):
            scores(2 * j + 2, 0, u)
            update(2 * j + 1, 1, u)
        return carry

    lax.fori_loop(0, i, body, 0)
    for u in range(n_u):
        if diag_keys(units[u][1], 1)[0] > 0:
            scores(2 * i + 1, 1, u)
        update(2 * i, 0, u, 0)
    for u in range(n_u):
        update(2 * i + 1, 1, u, 1)

    for qt in range(n_qt):
        rows = []
        for hh in range(2):
            acc = acc_ref[hh, qt]
            rows.append(acc[0:HEAD_DIM] * (1.0 / acc[HEAD_DIM:HEAD_DIM + 1]))
        pair_t = jnp.concatenate(rows, axis=0)
        o_ref[0, qt * FOX_TQ:(qt + 1) * FOX_TQ, :] = pair_t.T.astype(BF16)


def _fox_attention(qa, ka, vt):
    B, H, S, _ = qa.shape
    n_qt = FOX_BQ // FOX_TQ
    stage = pltpu.VMEM((2 * n_qt, FOX_TK, FOX_TQ), F32)
    stage_max = pltpu.VMEM((2 * n_qt, 1, FOX_TQ), F32)
    return pl.pallas_call(
        _fox_kernel,
        out_shape=jax.ShapeDtypeStruct((B, S, D_FOX), BF16),
        grid=(B, H // 2, S // FOX_BQ),
        in_specs=[pl.BlockSpec((1, 2, FOX_BQ, LANES), lambda b, p, i: (b, p, i, 0)),
                  pl.BlockSpec((1, 2, S, LANES), lambda b, p, i: (b, p, 0, 0)),
                  pl.BlockSpec((1, 2, S // FOX_TK, VT_ROWS, FOX_TK),
                               lambda b, p, i: (b, p, 0, 0, 0))],
        out_specs=pl.BlockSpec((1, FOX_BQ, LANES), lambda b, p, i: (b, i, p)),
        scratch_shapes=[pltpu.VMEM((2, n_qt, 1, FOX_TQ), F32),
                        pltpu.VMEM((2, n_qt, VT_ROWS, FOX_TQ), F32),
                        stage, stage, stage_max, stage_max],
        compiler_params=_cparams("arbitrary", "arbitrary", "arbitrary"),
        name="fox_attention",
    )(qa, ka, vt)


FF_CHUNK = MXU_DIM


def _mix_ffn_kernel(yr_ref, yg_ref, yf_ref, x_ref, gt1_ref, sc_ref, sh_ref, gt2_ref, g_ref,
                    wo_ref, wu_ref, cw_ref, cb_ref, wd_ref, o_ref, carry_ref, stage_ref, act_ref):
    tm = x_ref.shape[1]

    @pl.when(pl.program_id(1) == 0)
    def _():
        carry_ref[...] = jnp.zeros_like(carry_ref)

    halves = []
    for rs in (slice(0, tm // 2), slice(tm // 2, tm)):
        mix = jnp.concatenate([yr_ref[0, rs], yg_ref[0, rs], yf_ref[0, rs]], axis=-1)
        xn = x_ref[0, rs] + gt1_ref[0] * jnp.dot(mix, wo_ref[...], preferred_element_type=F32)
        o_ref[0, rs] = xn
        halves.append(xn)
    h = jnp.concatenate([_mod_rms(xn, g_ref[...], sc_ref[0], sh_ref[0]).astype(BF16)
                         for xn in halves], axis=0)

    def conv_cols(c0):
        cs = slice(c0, c0 + FF_CHUNK)
        up = jnp.dot(h, wu_ref[:, cs], preferred_element_type=F32)
        stage_ref[0:SUBLANES, :] = carry_ref[:, cs]
        stage_ref[SUBLANES:, :] = up
        carry_ref[:, cs] = up[tm - SUBLANES:, :]
        prev1 = stage_ref[SUBLANES - 1:SUBLANES - 1 + tm, :]
        prev2 = stage_ref[SUBLANES - 2:SUBLANES - 2 + tm, :]
        return (cw_ref[0, 2:3, cs] * up + cw_ref[0, 1:2, cs] * prev1 + cw_ref[0, 0:1, cs] * prev2
                + cb_ref[0, :, cs])

    for f in range(D_FF // FF_CHUNK):
        a = conv_cols(f * FF_CHUNK)
        b = conv_cols(D_FF + f * FF_CHUNK)
        act_ref[:, f * FF_CHUNK:(f + 1) * FF_CHUNK] = ((a * jax.nn.sigmoid(a)) * b).astype(BF16)
    down = jnp.dot(act_ref[...], wd_ref[...], preferred_element_type=F32)
    o_ref[0] = o_ref[0] + gt2_ref[0] * down


def _mix_ffn(yr, yg, yf, x, gt1, sc, sh, gt2, g, layer, w_o, w_up, conv_w, conv_b, w_down, tm):
    B, S, D = x.shape
    row = lambda n: pl.BlockSpec((1, tm, n), lambda b, i: (b, i, 0))
    vec = pl.BlockSpec((1, 1, D), lambda b, i: (b, 0, 0))
    return pl.pallas_call(
        _mix_ffn_kernel,
        out_shape=jax.ShapeDtypeStruct((B, S, D), F32),
        grid=(B, S // tm),
        in_specs=[row(D_RET), row(D_GM), row(D_FOX), row(D), vec, vec, vec, vec,
                  pl.BlockSpec((1, D), lambda b, i: (0, 0)),
                  _resident(w_o), _resident(w_up),
                  _layer_resident(conv_w, layer), _layer_resident(conv_b, layer),
                  _resident(w_down)],
        out_specs=row(D),
        scratch_shapes=[pltpu.VMEM((SUBLANES, 2 * D_FF), F32),
                        pltpu.VMEM((tm + SUBLANES, FF_CHUNK), F32),
                        pltpu.VMEM((tm, D_FF), BF16)],
        compiler_params=_cparams("arbitrary", "arbitrary"),
        name="mix_ffn",
    )(yr, yg, yf, x, gt1, sc, sh, gt2, g, w_o, w_up, conv_w, conv_b, w_down)


def kernel(x, c, ada_w, ada_b, norm1_g, w_in, ret_norm_g, gm_ln_g, gm_ln_b, gm_ws, gm_bs,
           fox_qn_g, fox_kn_g, fox_bf, w_o, norm2_g, w_up, conv_w, conv_b, w_down):
    B, S, D = x.shape
    L = ada_w.shape[0]
    assert D == D_MODEL and S % FOX_BQ == 0
    tm = 512

    mod = _modulation(c, ada_w, ada_b).reshape(L, B, 6, 1, D)
    cos_t, sin_t, dq, dk = _rope_tables(S)
    w_in_b = w_in.astype(BF16)
    conv_b3 = conv_b.reshape(L, 1, 2 * D_FF)

    for l in range(L):
        sh1, sc1, gt1, sh2, sc2, gt2 = (mod[l, :, i] for i in range(6))
        ws_cat = gm_ws[l].transpose(1, 0, 2).reshape(CHUNK, N_GM_HEADS * CHUNK)
        gate_bias = jnp.repeat(gm_bs[l].T, HEAD_DIM, axis=1)
        bf_row = jnp.pad(fox_bf[l], (FL_LANE0, 0)).reshape(1, LANES)
        p_ret, y_g, qa, ka, vt = _in_projection(
            x, sc1, sh1, norm1_g[l].reshape(1, D), w_in_b, l, cos_t, sin_t, dq, dk,
            gm_ln_g[l].reshape(1, D_GM), gm_ln_b[l].reshape(1, D_GM), ws_cat, gate_bias, bf_row,
            jnp.tile(fox_qn_g[l], N_FOX_HEADS).reshape(1, D_FOX),
            jnp.tile(fox_kn_g[l], N_FOX_HEADS).reshape(1, D_FOX), tm)

        y_r, w_o_b, w_up_b, w_down_b = _retention(p_ret, ret_norm_g[l].reshape(1, D_RET), l,
                                                   w_o, w_up, w_down, tm)
        y_f = _fox_attention(qa, ka, vt)

        x = _mix_ffn(y_r, y_g, y_f, x, gt1, sc2, sh2, gt2, norm2_g[l].reshape(1, D), l,
                     w_o_b, w_up_b, conv_w, conv_b3, w_down_b, tm)
    return x
```

```python
import functools
import math

import jax
import jax.numpy as jnp
from jax import lax
from jax.experimental import pallas as pl
from jax.experimental.pallas import tpu as pltpu

D_MODEL = 1024
HEAD_DIM = 64
N_RET_HEADS = 6
N_GM_HEADS = 4
N_FOX_HEADS = 6
D_RET = N_RET_HEADS * HEAD_DIM
D_GM = N_GM_HEADS * HEAD_DIM
D_FOX = N_FOX_HEADS * HEAD_DIM
CHUNK = 128
D_FF = 2816
ROPE_BASE = 10000.0
EPS = 1e-6
NEG_INF = -1e30

LANES = 128
SUBLANES = 8
MXU_DIM = 256
VMEM_LIMIT = 52 * 1024 * 1024

N_MAIN = 4 * D_RET + 2 * D_GM + 3 * D_FOX
GM_COL0 = 4 * D_RET
FOX_COL0 = GM_COL0 + 2 * D_GM
N_IN = N_MAIN + N_FOX_HEADS
FL_LANE0 = LANES - N_FOX_HEADS

LOG2E = math.log2(math.e)
FOX_TK = 1024
FOX_TQ = MXU_DIM
FOX_BQ = 2 * FOX_TK
VT_ROWS = 80

F32 = jnp.float32
BF16 = jnp.bfloat16


def _cparams(*sem, vmem=VMEM_LIMIT):
    return pltpu.CompilerParams(dimension_semantics=sem, vmem_limit_bytes=vmem)


def _resident(whole):
    nd = whole.ndim
    return pl.BlockSpec(whole.shape, lambda *_: (0,) * nd, pipeline_mode=pl.Buffered(1))


def _layer_resident(stacked, layer):
    nd = stacked.ndim - 1
    return pl.BlockSpec((1,) + stacked.shape[1:], lambda *_: (layer,) + (0,) * nd,
                        pipeline_mode=pl.Buffered(1))


def _lane_iota(shape):
    return lax.broadcasted_iota(jnp.int32, shape, len(shape) - 1)


def _mod_kernel(ct_ref, w_ref, b_ref, o_ref):
    ct = ct_ref[...]
    cond_t = ct * jax.nn.sigmoid(ct)
    w = w_ref[0]
    rows = [jnp.sum(cond_t[:, b:b + 1] * w, axis=0, keepdims=True)
            for b in range(ct.shape[1])]
    o_ref[0] = jnp.concatenate(rows, axis=0) + b_ref[0]


def _modulation(c, ada_w, ada_b):
    L, D, N = ada_w.shape
    B = c.shape[0]
    tn = 1536
    return pl.pallas_call(
        _mod_kernel,
        out_shape=jax.ShapeDtypeStruct((L, B, N), F32),
        grid=(L, N // tn),
        in_specs=[pl.BlockSpec((D, B), lambda l, j: (0, 0)),
                  pl.BlockSpec((1, D, tn), lambda l, j: (l, 0, j)),
                  pl.BlockSpec((1, 1, tn), lambda l, j: (l, 0, j))],
        out_specs=pl.BlockSpec((1, B, tn), lambda l, j: (l, 0, j)),
        compiler_params=_cparams("arbitrary", "arbitrary"),
        name="adaln_mod",
    )(c.T, ada_w, ada_b.reshape(L, 1, N))


def _ret_log_gamma(h):
    return math.log(1.0 - 2.0 ** (-5.0 - h))


def _rope_kernel(inv_ref, cos_ref, sin_ref, dq_ref, dk_ref):
    ts = cos_ref.shape[0]
    pos = (lax.broadcasted_iota(jnp.int32, (ts, LANES), 0) + pl.program_id(0) * ts).astype(F32)
    ang = pos * inv_ref[...]
    first_half = (_lane_iota((ts, LANES)) % HEAD_DIM) < (HEAD_DIM // 2)
    cos_ref[...] = jnp.cos(ang)
    s = jnp.sin(ang)
    sin_ref[...] = jnp.where(first_half, -s, s)

    head = _lane_iota((CHUNK, D_RET)) // HEAD_DIM
    lg = jnp.zeros((CHUNK, D_RET), F32)
    for h in range(N_RET_HEADS):
        lg = jnp.where(head == h, _ret_log_gamma(h), lg)
    t1 = lax.broadcasted_iota(jnp.int32, (CHUNK, D_RET), 0).astype(F32) + 1.0
    dq_ref[...] = jnp.exp(lg * t1)
    dk_ref[...] = jnp.exp(-lg * t1) * (HEAD_DIM ** -0.5)


def _rope_tables(S):
    half = HEAD_DIM // 2
    inv = ROPE_BASE ** (-jnp.arange(half, dtype=F32) / half)
    inv_row = jnp.tile(inv, LANES // half).reshape(1, LANES)
    ts = 512
    decay = jax.ShapeDtypeStruct((CHUNK, D_RET), F32)
    decay_spec = pl.BlockSpec((CHUNK, D_RET), lambda i: (0, 0))
    return pl.pallas_call(
        _rope_kernel,
        out_shape=(jax.ShapeDtypeStruct((S, LANES), F32),) * 2 + (decay, decay),
        grid=(S // ts,),
        in_specs=[pl.BlockSpec((1, LANES), lambda i: (0, 0))],
        out_specs=(pl.BlockSpec((ts, LANES), lambda i: (i, 0)),) * 2 + (decay_spec, decay_spec),
        compiler_params=_cparams("arbitrary"),
        name="rope_tables",
    )(inv_row)


def _mod_rms(x, g, sc, sh):
    ms = jnp.mean(x * x, axis=-1, keepdims=True)
    return (x * lax.rsqrt(ms + EPS) * g) * (1.0 + sc) + sh


def _swap_half_heads(x):
    first_half = (_lane_iota(x.shape) % HEAD_DIM) < (HEAD_DIM // 2)
    return jnp.where(first_half, pltpu.roll(x, LANES - HEAD_DIM // 2, 1),
                     pltpu.roll(x, HEAD_DIM // 2, 1))


def _gelu(x):
    return 0.5 * x * (1.0 + jnp.tanh(math.sqrt(2.0 / math.pi) * (x + 0.044715 * (x * x * x))))


def _pair_rms_scale(x):
    low = _lane_iota(x.shape) < HEAD_DIM
    sq = x * x
    ss0 = jnp.sum(jnp.where(low, sq, 0.0), axis=-1, keepdims=True)
    ss1 = jnp.sum(jnp.where(low, 0.0, sq), axis=-1, keepdims=True)
    return jnp.where(low, lax.rsqrt(ss0 * (1.0 / HEAD_DIM) + EPS),
                     lax.rsqrt(ss1 * (1.0 / HEAD_DIM) + EPS))


def _inproj_kernel(x_ref, xn_ref, sc_ref, sh_ref, g_ref, w_ref, cos_ref, sin_ref, dq_ref, dk_ref,
                   lng_ref, lnb_ref, ws_ref, gb_ref, bf_ref, qg_ref, kg_ref,
                   wo_ref, wu_ref, wd_ref,
                   pret_ref, yg_ref, qa_ref, ka_ref, vt_ref, wo_out, wu_out, wd_out,
                   h_a, h_b, nc_a, nc_b, carry_ref):
    tm = x_ref.shape[1]
    step = pl.program_id(1)
    wo_out[...] = wo_ref[0].astype(BF16)
    wu_out[...] = wu_ref[0].astype(BF16)
    wd_out[...] = wd_ref[0].astype(BF16)

    def prepare(x_tile_ref, h_out, nc_out):
        stages, box = [], {}

        def norm_rows(q):
            rs = slice(q * CHUNK, (q + 1) * CHUNK)
            h_out[rs, :] = _mod_rms(x_tile_ref[0, rs], g_ref[...], sc_ref[0], sh_ref[0]).astype(BF16)

        def forget_logits():
            z = jnp.dot(h_out[...], w_ref[0, :, N_IN - LANES:N_IN],
                        preferred_element_type=F32) + bf_ref[...]
            box["logf"] = jnp.minimum(z, 0.0) - jnp.log1p(jnp.exp(-jnp.abs(z)))

        def cumulate():
            tri = (lax.broadcasted_iota(jnp.int32, (CHUNK, CHUNK), 1) <=
                   lax.broadcasted_iota(jnp.int32, (CHUNK, CHUNK), 0)).astype(F32)
            logf = box["logf"]
            within = [jnp.dot(tri, logf[c * CHUNK:(c + 1) * CHUNK],
                              precision=lax.Precision.HIGHEST, preferred_element_type=F32)
                      for c in range(tm // CHUNK)]
            carry = carry_ref[0:1, :]
            for c, cum in enumerate(within):
                nc_out[c * CHUNK:(c + 1) * CHUNK, :] = (cum + carry) * (-LOG2E)
                carry = carry + cum[CHUNK - 1:CHUNK, :]
            carry_ref[...] = jnp.broadcast_to(carry, carry_ref.shape)

        for q in range(tm // CHUNK):
            stages.append(functools.partial(norm_rows, q))
        return stages + [forget_logits, cumulate]

    @pl.when(step == 0)
    def _():
        carry_ref[...] = jnp.zeros_like(carry_ref)
        for stage in prepare(x_ref, h_a, nc_a):
            stage()

    @pl.when(step % 2 == 0)
    def _():
        _inproj_tile(h_a, nc_a, prepare(xn_ref, h_b, nc_b), w_ref, cos_ref, sin_ref,
                     dq_ref, dk_ref, lng_ref, lnb_ref, ws_ref, gb_ref, qg_ref, kg_ref,
                     pret_ref, yg_ref, qa_ref, ka_ref, vt_ref)

    @pl.when(step % 2 == 1)
    def _():
        _inproj_tile(h_b, nc_b, prepare(xn_ref, h_a, nc_a), w_ref, cos_ref, sin_ref,
                     dq_ref, dk_ref, lng_ref, lnb_ref, ws_ref, gb_ref, qg_ref, kg_ref,
                     pret_ref, yg_ref, qa_ref, ka_ref, vt_ref)


def _inproj_tile(h_ref, nc_ref, next_stages, w_ref, cos_ref, sin_ref, dq_ref, dk_ref, lng_ref,
                 lnb_ref, ws_ref, gb_ref, qg_ref, kg_ref, pret_ref, yg_ref, qa_ref, ka_ref, vt_ref):
    tm = h_ref.shape[0]
    h = h_ref[...]
    neg_cum = nc_ref[...]
    bias_hi = neg_cum.astype(BF16).astype(F32)
    rest = neg_cum - bias_hi
    bias_mid = rest.astype(BF16).astype(F32)
    bias_lo = (rest - bias_mid).astype(BF16).astype(F32)

    lane = _lane_iota((tm, LANES))
    low = lane < HEAD_DIM
    ones_row = jnp.where(lax.broadcasted_iota(jnp.int32, (VT_ROWS - HEAD_DIM, tm), 0) == 0,
                         1.0, 0.0).astype(BF16)

    def fox_tiles(piece, kind, pair):
        if kind == 2:
            v_t = piece.T.astype(BF16)
        else:
            cols = slice(pair * LANES, (pair + 1) * LANES)
            gain = qg_ref[:, cols] * (HEAD_DIM ** -0.5 * LOG2E) if kind == 0 else kg_ref[:, cols]
            normed = piece * _pair_rms_scale(piece) * gain
        for half in range(2):
            hd = 2 * pair + half
            own = low if half == 0 else jnp.logical_not(low)
            b0 = HEAD_DIM * (1 - half)
            if kind == 0:
                ones = jnp.where((lane >= b0) & (lane < b0 + 3), 1.0, 0.0)
                qa_ref[0, hd] = jnp.where(own, normed, ones).astype(BF16)
            elif kind == 1:
                col = slice(FL_LANE0 + hd, FL_LANE0 + hd + 1)
                bias = jnp.where(lane == b0, bias_hi[:, col],
                                 jnp.where(lane == b0 + 1, bias_mid[:, col],
                                           jnp.where(lane == b0 + 2, bias_lo[:, col], 0.0)))
                ka_ref[0, hd] = jnp.where(own, normed, bias).astype(BF16)
            else:
                vt_ref[0, hd, 0, 0:HEAD_DIM, :] = v_t[half * HEAD_DIM:(half + 1) * HEAD_DIM]
                vt_ref[0, hd, 0, HEAD_DIM:VT_ROWS, :] = ones_row

    def spatial_gate(u, vn):
        lane_c = _lane_iota((CHUNK, LANES))
        keep_a = jnp.where(lane_c < HEAD_DIM, 1.0, 0.0).astype(BF16)
        keep_b = jnp.where(lane_c < HEAD_DIM, 0.0, 1.0).astype(BF16)
        row = lax.broadcasted_iota(jnp.int32, (CHUNK, 2 * CHUNK), 0)
        col = _lane_iota((CHUNK, 2 * CHUNK)) % CHUNK
        vb = vn.astype(BF16)
        for pair in range(N_GM_HEADS // 2):
            w_pair = jnp.where(col <= row, ws_ref[:, pair * 2 * CHUNK:(pair + 1) * 2 * CHUNK],
                               0.0).astype(BF16)
            ls = slice(pair * LANES, (pair + 1) * LANES)
            for c in range(tm // CHUNK):
                rs = slice(c * CHUNK, (c + 1) * CHUNK)
                v_tile = vb[rs, ls]
                stacked = jnp.concatenate([v_tile * keep_a, v_tile * keep_b], axis=0)
                mixed = jnp.dot(w_pair, stacked, preferred_element_type=F32)
                yg_ref[0, rs, ls] = (u[rs, ls] * (mixed + gb_ref[:, ls])).astype(BF16)

    def rotary_decay(piece, table_ref, tile):
        rot = piece * cos_ref[...] + _swap_half_heads(piece) * sin_ref[...]
        table = table_ref[:, tile * LANES:(tile + 1) * LANES]
        return rot * jnp.concatenate([table] * (tm // CHUNK), axis=0)

    for lo in range(0, N_MAIN, MXU_DIM):
        width = min(MXU_DIM, N_MAIN - lo)
        r = jnp.dot(h, w_ref[0, :, lo:lo + width], preferred_element_type=F32)
        if next_stages:
            next_stages.pop(0)()
        if lo == GM_COL0:
            gate_u = _gelu(r)
            continue
        if lo == GM_COL0 + D_GM:
            v = _gelu(r)
            mu = jnp.mean(v, axis=-1, keepdims=True)
            vc = v - mu
            var = jnp.mean(vc * vc, axis=-1, keepdims=True)
            spatial_gate(gate_u, vc * lax.rsqrt(var + EPS) * lng_ref[...] + lnb_ref[...])
            continue
        for off in range(0, width, LANES):
            c0 = lo + off
            piece = r[:, off:off + LANES]
            if c0 < D_RET:
                piece = rotary_decay(piece, dq_ref, c0 // LANES)
            elif c0 < 2 * D_RET:
                piece = rotary_decay(piece, dk_ref, (c0 - D_RET) // LANES)
            elif 3 * D_RET <= c0 < 4 * D_RET:
                piece = piece * jax.nn.sigmoid(piece)
            if c0 < GM_COL0:
                pret_ref[0, :, c0:c0 + LANES] = piece.astype(BF16)
            else:
                kind, pair = divmod((c0 - FOX_COL0) // LANES, N_FOX_HEADS // 2)
                fox_tiles(piece, kind, pair)


def _in_projection(x, sc, sh, g, w_in, layer, cos_t, sin_t, dq, dk, ln_g, ln_b, ws_cat,
                   gate_bias, bf_row, qg, kg, w_o, w_up, w_down, tm):
    B, S, D = x.shape
    steps = B * (S // tm)

    def slab(w, n_slabs):
        rows = w.shape[1] // n_slabs
        assert rows * n_slabs == w.shape[1] and rows % 16 == 0 and steps % n_slabs == 0
        rep = steps // n_slabs
        return pl.BlockSpec((1, rows) + w.shape[2:],
                            lambda b, i: (layer, (b * n_s + i) // rep, 0))

    def slab_out(w, n_slabs):
        rows = w.shape[1] // n_slabs
        rep = steps // n_slabs
        return pl.BlockSpec((rows,) + w.shape[2:], lambda b, i: ((b * n_s + i) // rep, 0))

    slabs = ((w_o, steps), (w_up, steps), (w_down, steps // 2))
    assert FOX_TK % tm == 0 and (S // tm) % 2 == 0
    per_kb = FOX_TK // tm
    n_s = S // tm
    const = lambda shape: pl.BlockSpec(shape, lambda b, i: (0,) * len(shape))
    aug = jax.ShapeDtypeStruct((B, N_FOX_HEADS, S, LANES), BF16)
    aug_spec = pl.BlockSpec((1, N_FOX_HEADS, tm, LANES), lambda b, i: (b, 0, i, 0))
    return pl.pallas_call(
        _inproj_kernel,
        out_shape=(jax.ShapeDtypeStruct((B, S, 4 * D_RET), BF16),
                   jax.ShapeDtypeStruct((B, S, D_GM), BF16),
                   aug, aug,
                   jax.ShapeDtypeStruct((B, N_FOX_HEADS, S // FOX_TK, VT_ROWS, FOX_TK), BF16))
        + tuple(jax.ShapeDtypeStruct(w.shape[1:], BF16) for w, _ in slabs),
        grid=(B, S // tm),
        in_specs=[pl.BlockSpec((1, tm, D), lambda b, i: (b, i, 0)),
                  pl.BlockSpec((1, tm, D), lambda b, i: (b, jnp.minimum(i + 1, n_s - 1), 0)),
                  pl.BlockSpec((1, 1, D), lambda b, i: (b, 0, 0)),
                  pl.BlockSpec((1, 1, D), lambda b, i: (b, 0, 0)),
                  const((1, D)),
                  _layer_resident(w_in, layer),
                  pl.BlockSpec((tm, LANES), lambda b, i: (i, 0)),
                  pl.BlockSpec((tm, LANES), lambda b, i: (i, 0)),
                  const((CHUNK, D_RET)), const((CHUNK, D_RET)),
                  const((1, D_GM)), const((1, D_GM)),
                  const((CHUNK, N_GM_HEADS * CHUNK)), const((CHUNK, D_GM)),
                  const((1, LANES)), const((1, D_FOX)), const((1, D_FOX))]
        + [slab(w, n) for w, n in slabs],
        out_specs=(pl.BlockSpec((1, tm, 4 * D_RET), lambda b, i: (b, i, 0)),
                   pl.BlockSpec((1, tm, D_GM), lambda b, i: (b, i, 0)),
                   aug_spec, aug_spec,
                   pl.BlockSpec((1, N_FOX_HEADS, 1, VT_ROWS, tm),
                                lambda b, i: (b, 0, i // per_kb, 0, i % per_kb)))
        + tuple(slab_out(w, n) for w, n in slabs),
        scratch_shapes=[pltpu.VMEM((tm, D), BF16), pltpu.VMEM((tm, D), BF16),
                        pltpu.VMEM((tm, LANES), F32), pltpu.VMEM((tm, LANES), F32),
                        pltpu.VMEM((SUBLANES, LANES), F32)],
        compiler_params=_cparams("arbitrary", "arbitrary"),
        name="in_proj",
    )(x, x, sc, sh, g, w_in, cos_t, sin_t, dq, dk, ln_g, ln_b, ws_cat, gate_bias, bf_row, qg, kg,
      w_o, w_up, w_down)


def _retention_tasks(q_ref, k_ref, v_ref, g_ref, ng_ref, dq_ref, o_ref, state_ref, first_block):
    n_c = q_ref.shape[1] // CHUNK
    low = _lane_iota((CHUNK, LANES)) < HEAD_DIM
    keeps = (jnp.where(low, 1.0, 0.0).astype(BF16), jnp.where(low, 0.0, 1.0).astype(BF16))
    causal = (lax.broadcasted_iota(jnp.int32, (CHUNK, CHUNK), 0) >=
              lax.broadcasted_iota(jnp.int32, (CHUNK, CHUNK), 1))
    blockdiag = (lax.broadcasted_iota(jnp.int32, (LANES, LANES), 0) < HEAD_DIM) == low
    dn_t = (((1,), (1,)), ((), ()))
    dn_ta = (((0,), (0,)), ((), ()))
    kvs, scores, states = {}, {}, {}
    rows = lambda c: slice(c * CHUNK, (c + 1) * CHUNK)

    def chunk_products(c):
        q, k, v = q_ref[0, rows(c), :], k_ref[0, rows(c), :], v_ref[0, rows(c), :]
        kvs[c] = lax.dot_general(k, v, dn_ta, preferred_element_type=F32)
        for hh, keep in enumerate(keeps):
            s = lax.dot_general(q * keep, k, dn_t, preferred_element_type=F32)
            scores[c, hh] = jnp.where(causal, s, 0.0).astype(BF16)

    def recurrence():
        chunk_decay = dq_ref[CHUNK - 1:CHUNK, :]
        state = jnp.where(first_block, 0.0, state_ref[...])
        for c in range(n_c):
            states[c] = state.astype(BF16)
            state = (state + jnp.where(blockdiag, kvs[c], 0.0)) * chunk_decay
        state_ref[...] = state

    def chunk_output(c):
        q = q_ref[0, rows(c), :]
        rhs = jnp.concatenate([v_ref[0, rows(c), :], states[c]], axis=0)
        ys = [jnp.dot(jnp.concatenate([scores[c, hh], q * keep], axis=1), rhs,
                      preferred_element_type=F32) for hh, keep in enumerate(keeps)]
        y = jnp.where(low, ys[0], ys[1])
        yn = y * _pair_rms_scale(y) * ng_ref[0]
        o_ref[0, rows(c), :] = (yn * g_ref[0, rows(c), :].astype(F32)).astype(BF16)

    return ([functools.partial(chunk_products, c) for c in range(n_c)] + [recurrence]
            + [functools.partial(chunk_output, c) for c in range(n_c)])


def _fox_kernel(q_ref, k_ref, vt_ref, rq_ref, rk_ref, rv_ref, rg_ref, ng_ref, dq_ref,
                o_ref, yr_ref, m_ref, acc_ref, s0_ref, s1_ref, mb0_ref, mb1_ref, state_ref):
    s_refs = (s0_ref, s1_ref)
    mb_refs = (mb0_ref, mb1_ref)
    i = pl.program_id(2)
    n_qt = FOX_BQ // FOX_TQ
    units = [(hh, qt) for hh in range(2) for qt in range(n_qt)]
    n_u = len(units)
    dn_t = (((1,), (1,)), ((), ()))
    m_ref[...] = jnp.full(m_ref.shape, NEG_INF, F32)
    acc_ref[...] = jnp.zeros_like(acc_ref)
    tasks = _retention_tasks(rq_ref, rk_ref, rv_ref, rg_ref, ng_ref, dq_ref, yr_ref, state_ref,
                             i == 0)

    def side_task():
        if tasks:
            tasks.pop(0)()

    def diag_keys(qt, half):
        d = qt * FOX_TQ - half * FOX_TK
        nk = min(max(d + FOX_TQ, 0), FOX_TK)
        return nk, (d if nk - 1 > d else None)

    def scores(j, slot, u):
        hh, qt = units[u]
        start = pl.multiple_of(j * FOX_TK, FOX_TK)
        k = k_ref[0, hh, pl.ds(start, FOX_TK), :]
        q = q_ref[0, hh, qt * FOX_TQ:(qt + 1) * FOX_TQ, :]
        s = lax.dot_general(k, q, dn_t, preferred_element_type=F32)
        s_refs[slot][u] = s
        mb_refs[slot][u] = jnp.max(s, axis=0, keepdims=True)

    def update(j, slot, u, diag_half=None):
        hh, qt = units[u]
        nk, mask_off = (FOX_TK, None) if diag_half is None else diag_keys(qt, diag_half)
        if nk == 0:
            return
        s = s_refs[slot][u, 0:nk, :]
        m_blk = mb_refs[slot][u]
        if mask_off is not None or nk < FOX_TK:
            if mask_off is not None:
                key = lax.broadcasted_iota(jnp.int32, s.shape, 0)
                qry = lax.broadcasted_iota(jnp.int32, s.shape, 1)
                s = jnp.where(key <= qry + mask_off, s, NEG_INF)
            m_blk = jnp.max(s, axis=0, keepdims=True)
        m_old = m_ref[hh, qt]
        m_new = jnp.maximum(m_old, m_blk)
        p = jnp.exp2((s - m_new).astype(BF16))
        vt = vt_ref[0, hh, j, :, 0:nk]
        acc_ref[hh, qt] = jnp.exp2(m_old - m_new) * acc_ref[hh, qt] + jnp.dot(
            vt, p, preferred_element_type=F32)
        m_ref[hh, qt] = m_new

    for u in range(n_u):
        scores(0, 0, u)
        side_task()

    def body(j, carry):
        for u in range(n_u):
            scores(2 * j + 1, 1, u)
            update(2 * j, 0, u)
        for u in range(n_u):
            scores(2 * j + 2, 0, u)
            update(2 * j + 1, 1, u)
        return carry

    lax.fori_loop(0, i, body, 0)
    for u in range(n_u):
        if diag_keys(units[u][1], 1)[0] > 0:
            scores(2 * i + 1, 1, u)
        side_task()
        update(2 * i, 0, u, 0)
        side_task()
    for u in range(n_u):
        update(2 * i + 1, 1, u, 1)
        side_task()
    while tasks:
        side_task()

    for qt in range(n_qt):
        rows = []
        for hh in range(2):
            acc = acc_ref[hh, qt]
            rows.append(acc[0:HEAD_DIM] * (1.0 / acc[HEAD_DIM:HEAD_DIM + 1]))
        pair_t = jnp.concatenate(rows, axis=0)
        o_ref[0, qt * FOX_TQ:(qt + 1) * FOX_TQ, :] = pair_t.T.astype(BF16)


def _fox_attention(qa, ka, vt, p_ret, ng, dq):
    B, H, S, _ = qa.shape
    assert N_RET_HEADS == H
    n_qt = FOX_BQ // FOX_TQ
    stage = pltpu.VMEM((2 * n_qt, FOX_TK, FOX_TQ), F32)
    stage_max = pltpu.VMEM((2 * n_qt, 1, FOX_TQ), F32)
    n_pairs = H // 2
    ret_part = lambda kind: pl.BlockSpec((1, FOX_BQ, LANES),
                                         lambda b, p, i, kind=kind: (b, i, kind * n_pairs + p))
    y_spec = pl.BlockSpec((1, FOX_BQ, LANES), lambda b, p, i: (b, i, p))
    y_shape = jax.ShapeDtypeStruct((B, S, D_FOX), BF16)
    return pl.pallas_call(
        _fox_kernel,
        out_shape=(y_shape, y_shape),
        grid=(B, n_pairs, S // FOX_BQ),
        in_specs=[pl.BlockSpec((1, 2, FOX_BQ, LANES), lambda b, p, i: (b, p, i, 0)),
                  pl.BlockSpec((1, 2, S, LANES), lambda b, p, i: (b, p, 0, 0)),
                  pl.BlockSpec((1, 2, S // FOX_TK, VT_ROWS, FOX_TK),
                               lambda b, p, i: (b, p, 0, 0, 0)),
                  ret_part(0), ret_part(1), ret_part(2), ret_part(3),
                  pl.BlockSpec((1, 1, LANES), lambda b, p, i: (p, 0, 0)),
                  pl.BlockSpec((CHUNK, LANES), lambda b, p, i: (0, p))],
        out_specs=(y_spec, y_spec),
        scratch_shapes=[pltpu.VMEM((2, n_qt, 1, FOX_TQ), F32),
                        pltpu.VMEM((2, n_qt, VT_ROWS, FOX_TQ), F32),
                        stage, stage, stage_max, stage_max,
                        pltpu.VMEM((LANES, LANES), F32)],
        compiler_params=_cparams("arbitrary", "arbitrary", "arbitrary", vmem=60 * 1024 * 1024),
        name="fox_ret_attention",
    )(qa, ka, vt, p_ret, p_ret, p_ret, p_ret, ng, dq)


FF_CHUNK = MXU_DIM


def _mix_ffn_kernel(yr_ref, yg_ref, yf_ref, x_ref, gt1_ref, sc_ref, sh_ref, gt2_ref, g_ref,
                    wo_ref, wu_ref, cw_ref, cb_ref, wd_ref, o_ref, carry_ref, stage_ref, act_ref):
    tm = x_ref.shape[1]

    @pl.when(pl.program_id(1) == 0)
    def _():
        carry_ref[...] = jnp.zeros_like(carry_ref)

    halves = []
    for rs in (slice(0, tm // 2), slice(tm // 2, tm)):
        mix = jnp.concatenate([yr_ref[0, rs], yg_ref[0, rs], yf_ref[0, rs]], axis=-1)
        xn = x_ref[0, rs] + gt1_ref[0] * jnp.dot(mix, wo_ref[...], preferred_element_type=F32)
        o_ref[0, rs] = xn
        halves.append(xn)
    h = jnp.concatenate([_mod_rms(xn, g_ref[...], sc_ref[0], sh_ref[0]).astype(BF16)
                         for xn in halves], axis=0)

    def conv_cols(c0):
        cs = slice(c0, c0 + FF_CHUNK)
        up = jnp.dot(h, wu_ref[:, cs], preferred_element_type=F32)
        stage_ref[0:SUBLANES, :] = carry_ref[:, cs]
        stage_ref[SUBLANES:, :] = up
        carry_ref[:, cs] = up[tm - SUBLANES:, :]
        prev1 = stage_ref[SUBLANES - 1:SUBLANES - 1 + tm, :]
        prev2 = stage_ref[SUBLANES - 2:SUBLANES - 2 + tm, :]
        return (cw_ref[0, 2:3, cs] * up + cw_ref[0, 1:2, cs] * prev1 + cw_ref[0, 0:1, cs] * prev2
                + cb_ref[0, :, cs])

    for f in range(D_FF // FF_CHUNK):
        a = conv_cols(f * FF_CHUNK)
        b = conv_cols(D_FF + f * FF_CHUNK)
        act_ref[:, f * FF_CHUNK:(f + 1) * FF_CHUNK] = ((a * jax.nn.sigmoid(a)) * b).astype(BF16)
    down = jnp.dot(act_ref[...], wd_ref[...], preferred_element_type=F32)
    o_ref[0] = o_ref[0] + gt2_ref[0] * down


def _mix_ffn(yr, yg, yf, x, gt1, sc, sh, gt2, g, layer, w_o, w_up, conv_w, conv_b, w_down, tm):
    B, S, D = x.shape
    row = lambda n: pl.BlockSpec((1, tm, n), lambda b, i: (b, i, 0))
    vec = pl.BlockSpec((1, 1, D), lambda b, i: (b, 0, 0))
    return pl.pallas_call(
        _mix_ffn_kernel,
        out_shape=jax.ShapeDtypeStruct((B, S, D), F32),
        grid=(B, S // tm),
        in_specs=[row(D_RET), row(D_GM), row(D_FOX), row(D), vec, vec, vec, vec,
                  pl.BlockSpec((1, D), lambda b, i: (0, 0)),
                  _resident(w_o), _resident(w_up),
                  _layer_resident(conv_w, layer), _layer_resident(conv_b, layer),
                  _resident(w_down)],
        out_specs=row(D),
        scratch_shapes=[pltpu.VMEM((SUBLANES, 2 * D_FF), F32),
                        pltpu.VMEM((tm + SUBLANES, FF_CHUNK), F32),
                        pltpu.VMEM((tm, D_FF), BF16)],
        compiler_params=_cparams("arbitrary", "arbitrary"),
        name="mix_ffn",
    )(yr, yg, yf, x, gt1, sc, sh, gt2, g, w_o, w_up, conv_w, conv_b, w_down)


def kernel(x, c, ada_w, ada_b, norm1_g, w_in, ret_norm_g, gm_ln_g, gm_ln_b, gm_ws, gm_bs,
           fox_qn_g, fox_kn_g, fox_bf, w_o, norm2_g, w_up, conv_w, conv_b, w_down):
    B, S, D = x.shape
    L = ada_w.shape[0]
    assert D == D_MODEL and S % FOX_BQ == 0
    tm = 512

    mod = _modulation(c, ada_w, ada_b).reshape(L, B, 6, 1, D)
    cos_t, sin_t, dq, dk = _rope_tables(S)
    w_in_b = w_in.astype(BF16)
    conv_b3 = conv_b.reshape(L, 1, 2 * D_FF)

    for l in range(L):
        sh1, sc1, gt1, sh2, sc2, gt2 = (mod[l, :, i] for i in range(6))
        ws_cat = gm_ws[l].transpose(1, 0, 2).reshape(CHUNK, N_GM_HEADS * CHUNK)
        gate_bias = jnp.repeat(gm_bs[l].T, HEAD_DIM, axis=1)
        bf_row = jnp.pad(fox_bf[l], (FL_LANE0, 0)).reshape(1, LANES)
        p_ret, y_g, qa, ka, vt, w_o_b, w_up_b, w_down_b = _in_projection(
            x, sc1, sh1, norm1_g[l].reshape(1, D), w_in_b, l, cos_t, sin_t, dq, dk,
            gm_ln_g[l].reshape(1, D_GM), gm_ln_b[l].reshape(1, D_GM), ws_cat, gate_bias, bf_row,
            jnp.tile(fox_qn_g[l], N_FOX_HEADS).reshape(1, D_FOX),
            jnp.tile(fox_kn_g[l], N_FOX_HEADS).reshape(1, D_FOX), w_o, w_up, w_down, tm)

        y_f, y_r = _fox_attention(qa, ka, vt, p_ret,
                                  ret_norm_g[l].reshape(N_RET_HEADS // 2, 1, LANES), dq)

        x = _mix_ffn(y_r, y_g, y_f, x, gt1, sc2, sh2, gt2, norm2_g[l].reshape(1, D), l,
                     w_o_b, w_up_b, conv_w, conv_b3, w_down_b, tm)
    return x
```

```python
import functools
import math

import jax
import jax.numpy as jnp
from jax import lax
from jax.experimental import pallas as pl
from jax.experimental.pallas import tpu as pltpu

D_MODEL = 1024
HEAD_DIM = 64
N_RET_HEADS = 6
N_GM_HEADS = 4
N_FOX_HEADS = 6
D_RET = N_RET_HEADS * HEAD_DIM
D_GM = N_GM_HEADS * HEAD_DIM
D_FOX = N_FOX_HEADS * HEAD_DIM
CHUNK = 128
D_FF = 2816
ROPE_BASE = 10000.0
EPS = 1e-6
NEG_INF = -1e30

LANES = 128
SUBLANES = 8
MXU_DIM = 256
VMEM_LIMIT = 52 * 1024 * 1024

N_MAIN = 4 * D_RET + 2 * D_GM + 3 * D_FOX
GM_COL0 = 4 * D_RET
FOX_COL0 = GM_COL0 + 2 * D_GM
N_IN = N_MAIN + N_FOX_HEADS
FL_LANE0 = LANES - N_FOX_HEADS

LOG2E = math.log2(math.e)
FOX_TK = 1024
FOX_TQ = MXU_DIM
FOX_BQ = 2 * FOX_TK
VT_ROWS = 80

F32 = jnp.float32
BF16 = jnp.bfloat16


def _cparams(*sem):
    return pltpu.CompilerParams(dimension_semantics=sem, vmem_limit_bytes=VMEM_LIMIT)


def _resident(whole):
    nd = whole.ndim
    return pl.BlockSpec(whole.shape, lambda *_: (0,) * nd, pipeline_mode=pl.Buffered(1))


def _layer_resident(stacked, layer):
    nd = stacked.ndim - 1
    return pl.BlockSpec((1,) + stacked.shape[1:], lambda *_: (layer,) + (0,) * nd,
                        pipeline_mode=pl.Buffered(1))


def _lane_iota(shape):
    return lax.broadcasted_iota(jnp.int32, shape, len(shape) - 1)


def _mod_kernel(ct_ref, w_ref, b_ref, o_ref):
    ct = ct_ref[...]
    cond_t = ct * jax.nn.sigmoid(ct)
    w = w_ref[0]
    rows = [jnp.sum(cond_t[:, b:b + 1] * w, axis=0, keepdims=True)
            for b in range(ct.shape[1])]
    o_ref[0] = jnp.concatenate(rows, axis=0) + b_ref[0]


def _modulation(c, ada_w, ada_b):
    L, D, N = ada_w.shape
    B = c.shape[0]
    tn = 1536
    return pl.pallas_call(
        _mod_kernel,
        out_shape=jax.ShapeDtypeStruct((L, B, N), F32),
        grid=(L, N // tn),
        in_specs=[pl.BlockSpec((D, B), lambda l, j: (0, 0)),
                  pl.BlockSpec((1, D, tn), lambda l, j: (l, 0, j)),
                  pl.BlockSpec((1, 1, tn), lambda l, j: (l, 0, j))],
        out_specs=pl.BlockSpec((1, B, tn), lambda l, j: (l, 0, j)),
        compiler_params=_cparams("arbitrary", "arbitrary"),
        name="adaln_mod",
    )(c.T, ada_w, ada_b.reshape(L, 1, N))


def _ret_log_gamma(h):
    return math.log(1.0 - 2.0 ** (-5.0 - h))


def _rope_kernel(inv_ref, cos_ref, sin_ref, dq_ref, dk_ref):
    ts = cos_ref.shape[0]
    pos = (lax.broadcasted_iota(jnp.int32, (ts, LANES), 0) + pl.program_id(0) * ts).astype(F32)
    ang = pos * inv_ref[...]
    first_half = (_lane_iota((ts, LANES)) % HEAD_DIM) < (HEAD_DIM // 2)
    cos_ref[...] = jnp.cos(ang)
    s = jnp.sin(ang)
    sin_ref[...] = jnp.where(first_half, -s, s)

    head = _lane_iota((CHUNK, D_RET)) // HEAD_DIM
    lg = jnp.zeros((CHUNK, D_RET), F32)
    for h in range(N_RET_HEADS):
        lg = jnp.where(head == h, _ret_log_gamma(h), lg)
    t1 = lax.broadcasted_iota(jnp.int32, (CHUNK, D_RET), 0).astype(F32) + 1.0
    dq_ref[...] = jnp.exp(lg * t1)
    dk_ref[...] = jnp.exp(-lg * t1) * (HEAD_DIM ** -0.5)


def _rope_tables(S):
    half = HEAD_DIM // 2
    inv = ROPE_BASE ** (-jnp.arange(half, dtype=F32) / half)
    inv_row = jnp.tile(inv, LANES // half).reshape(1, LANES)
    ts = 512
    decay = jax.ShapeDtypeStruct((CHUNK, D_RET), F32)
    decay_spec = pl.BlockSpec((CHUNK, D_RET), lambda i: (0, 0))
    return pl.pallas_call(
        _rope_kernel,
        out_shape=(jax.ShapeDtypeStruct((S, LANES), F32),) * 2 + (decay, decay),
        grid=(S // ts,),
        in_specs=[pl.BlockSpec((1, LANES), lambda i: (0, 0))],
        out_specs=(pl.BlockSpec((ts, LANES), lambda i: (i, 0)),) * 2 + (decay_spec, decay_spec),
        compiler_params=_cparams("arbitrary"),
        name="rope_tables",
    )(inv_row)


def _mod_rms(x, g, sc, sh):
    ms = jnp.mean(x * x, axis=-1, keepdims=True)
    return (x * lax.rsqrt(ms + EPS) * g) * (1.0 + sc) + sh


def _swap_half_heads(x):
    first_half = (_lane_iota(x.shape) % HEAD_DIM) < (HEAD_DIM // 2)
    return jnp.where(first_half, pltpu.roll(x, LANES - HEAD_DIM // 2, 1),
                     pltpu.roll(x, HEAD_DIM // 2, 1))


def _gelu(x):
    return 0.5 * x * (1.0 + jnp.tanh(math.sqrt(2.0 / math.pi) * (x + 0.044715 * (x * x * x))))


def _pair_rms_scale(x):
    low = _lane_iota(x.shape) < HEAD_DIM
    sq = x * x
    ss0 = jnp.sum(jnp.where(low, sq, 0.0), axis=-1, keepdims=True)
    ss1 = jnp.sum(jnp.where(low, 0.0, sq), axis=-1, keepdims=True)
    return jnp.where(low, lax.rsqrt(ss0 * (1.0 / HEAD_DIM) + EPS),
                     lax.rsqrt(ss1 * (1.0 / HEAD_DIM) + EPS))


def _inproj_kernel(x_ref, xn_ref, sc_ref, sh_ref, g_ref, w_ref, cos_ref, sin_ref, dq_ref, dk_ref,
                   lng_ref, lnb_ref, ws_ref, gb_ref, bf_ref, qg_ref, kg_ref,
                   pret_ref, yg_ref, qa_ref, ka_ref, vt_ref,
                   h_a, h_b, nc_a, nc_b, carry_ref):
    tm = x_ref.shape[1]
    step = pl.program_id(1)

    def prepare(x_tile_ref, h_out, nc_out):
        stages, box = [], {}

        def norm_rows(q):
            rs = slice(q * CHUNK, (q + 1) * CHUNK)
            h_out[rs, :] = _mod_rms(x_tile_ref[0, rs], g_ref[...], sc_ref[0], sh_ref[0]).astype(BF16)

        def forget_logits():
            z = jnp.dot(h_out[...], w_ref[0, :, N_IN - LANES:N_IN],
                        preferred_element_type=F32) + bf_ref[...]
            box["logf"] = jnp.minimum(z, 0.0) - jnp.log1p(jnp.exp(-jnp.abs(z)))

        def cumulate():
            tri = (lax.broadcasted_iota(jnp.int32, (CHUNK, CHUNK), 1) <=
                   lax.broadcasted_iota(jnp.int32, (CHUNK, CHUNK), 0)).astype(F32)
            logf = box["logf"]
            within = [jnp.dot(tri, logf[c * CHUNK:(c + 1) * CHUNK],
                              precision=lax.Precision.HIGHEST, preferred_element_type=F32)
                      for c in range(tm // CHUNK)]
            carry = carry_ref[0:1, :]
            for c, cum in enumerate(within):
                nc_out[c * CHUNK:(c + 1) * CHUNK, :] = (cum + carry) * (-LOG2E)
                carry = carry + cum[CHUNK - 1:CHUNK, :]
            carry_ref[...] = jnp.broadcast_to(carry, carry_ref.shape)

        for q in range(tm // CHUNK):
            stages.append(functools.partial(norm_rows, q))
        return stages + [forget_logits, cumulate]

    @pl.when(step == 0)
    def _():
        carry_ref[...] = jnp.zeros_like(carry_ref)
        for stage in prepare(x_ref, h_a, nc_a):
            stage()

    @pl.when(step % 2 == 0)
    def _():
        _inproj_tile(h_a, nc_a, prepare(xn_ref, h_b, nc_b), w_ref, cos_ref, sin_ref,
                     dq_ref, dk_ref, lng_ref, lnb_ref, ws_ref, gb_ref, qg_ref, kg_ref,
                     pret_ref, yg_ref, qa_ref, ka_ref, vt_ref)

    @pl.when(step % 2 == 1)
    def _():
        _inproj_tile(h_b, nc_b, prepare(xn_ref, h_a, nc_a), w_ref, cos_ref, sin_ref,
                     dq_ref, dk_ref, lng_ref, lnb_ref, ws_ref, gb_ref, qg_ref, kg_ref,
                     pret_ref, yg_ref, qa_ref, ka_ref, vt_ref)


def _inproj_tile(h_ref, nc_ref, next_stages, w_ref, cos_ref, sin_ref, dq_ref, dk_ref, lng_ref,
                 lnb_ref, ws_ref, gb_ref, qg_ref, kg_ref, pret_ref, yg_ref, qa_ref, ka_ref, vt_ref):
    tm = h_ref.shape[0]
    h = h_ref[...]
    neg_cum = nc_ref[...]
    bias_hi = neg_cum.astype(BF16).astype(F32)
    rest = neg_cum - bias_hi
    bias_mid = rest.astype(BF16).astype(F32)
    bias_lo = (rest - bias_mid).astype(BF16).astype(F32)

    lane = _lane_iota((tm, LANES))
    low = lane < HEAD_DIM
    ones_row = jnp.where(lax.broadcasted_iota(jnp.int32, (VT_ROWS - HEAD_DIM, tm), 0) == 0,
                         1.0, 0.0).astype(BF16)

    def fox_tiles(piece, kind, pair):
        if kind == 2:
            v_t = piece.T.astype(BF16)
        else:
            cols = slice(pair * LANES, (pair + 1) * LANES)
            gain = qg_ref[:, cols] * (HEAD_DIM ** -0.5 * LOG2E) if kind == 0 else kg_ref[:, cols]
            normed = piece * _pair_rms_scale(piece) * gain
        for half in range(2):
            hd = 2 * pair + half
            own = low if half == 0 else jnp.logical_not(low)
            b0 = HEAD_DIM * (1 - half)
            if kind == 0:
                ones = jnp.where((lane >= b0) & (lane < b0 + 3), 1.0, 0.0)
                qa_ref[0, hd] = jnp.where(own, normed, ones).astype(BF16)
            elif kind == 1:
                col = slice(FL_LANE0 + hd, FL_LANE0 + hd + 1)
                bias = jnp.where(lane == b0, bias_hi[:, col],
                                 jnp.where(lane == b0 + 1, bias_mid[:, col],
                                           jnp.where(lane == b0 + 2, bias_lo[:, col], 0.0)))
                ka_ref[0, hd] = jnp.where(own, normed, bias).astype(BF16)
            else:
                vt_ref[0, hd, 0, 0:HEAD_DIM, :] = v_t[half * HEAD_DIM:(half + 1) * HEAD_DIM]
                vt_ref[0, hd, 0, HEAD_DIM:VT_ROWS, :] = ones_row

    def spatial_gate(u, vn):
        lane_c = _lane_iota((CHUNK, LANES))
        keep_a = jnp.where(lane_c < HEAD_DIM, 1.0, 0.0).astype(BF16)
        keep_b = jnp.where(lane_c < HEAD_DIM, 0.0, 1.0).astype(BF16)
        row = lax.broadcasted_iota(jnp.int32, (CHUNK, 2 * CHUNK), 0)
        col = _lane_iota((CHUNK, 2 * CHUNK)) % CHUNK
        vb = vn.astype(BF16)
        for pair in range(N_GM_HEADS // 2):
            w_pair = jnp.where(col <= row, ws_ref[:, pair * 2 * CHUNK:(pair + 1) * 2 * CHUNK],
                               0.0).astype(BF16)
            ls = slice(pair * LANES, (pair + 1) * LANES)
            for c in range(tm // CHUNK):
                rs = slice(c * CHUNK, (c + 1) * CHUNK)
                v_tile = vb[rs, ls]
                stacked = jnp.concatenate([v_tile * keep_a, v_tile * keep_b], axis=0)
                mixed = jnp.dot(w_pair, stacked, preferred_element_type=F32)
                yg_ref[0, rs, ls] = (u[rs, ls] * (mixed + gb_ref[:, ls])).astype(BF16)

    def rotary_decay(piece, table_ref, tile):
        rot = piece * cos_ref[...] + _swap_half_heads(piece) * sin_ref[...]
        table = table_ref[:, tile * LANES:(tile + 1) * LANES]
        return rot * jnp.concatenate([table] * (tm // CHUNK), axis=0)

    for lo in range(0, N_MAIN, MXU_DIM):
        width = min(MXU_DIM, N_MAIN - lo)
        r = jnp.dot(h, w_ref[0, :, lo:lo + width], preferred_element_type=F32)
        if next_stages:
            next_stages.pop(0)()
        if lo == GM_COL0:
            gate_u = _gelu(r)
            continue
        if lo == GM_COL0 + D_GM:
            v = _gelu(r)
            mu = jnp.mean(v, axis=-1, keepdims=True)
            vc = v - mu
            var = jnp.mean(vc * vc, axis=-1, keepdims=True)
            spatial_gate(gate_u, vc * lax.rsqrt(var + EPS) * lng_ref[...] + lnb_ref[...])
            continue
        for off in range(0, width, LANES):
            c0 = lo + off
            piece = r[:, off:off + LANES]
            if c0 < D_RET:
                piece = rotary_decay(piece, dq_ref, c0 // LANES)
            elif c0 < 2 * D_RET:
                piece = rotary_decay(piece, dk_ref, (c0 - D_RET) // LANES)
            elif 3 * D_RET <= c0 < 4 * D_RET:
                piece = piece * jax.nn.sigmoid(piece)
            if c0 < GM_COL0:
                pret_ref[0, :, c0:c0 + LANES] = piece.astype(BF16)
            else:
                kind, pair = divmod((c0 - FOX_COL0) // LANES, N_FOX_HEADS // 2)
                fox_tiles(piece, kind, pair)


def _in_projection(x, sc, sh, g, w_in, layer, cos_t, sin_t, dq, dk, ln_g, ln_b, ws_cat,
                   gate_bias, bf_row, qg, kg, tm):
    B, S, D = x.shape
    assert FOX_TK % tm == 0 and (S // tm) % 2 == 0
    per_kb = FOX_TK // tm
    n_s = S // tm
    const = lambda shape: pl.BlockSpec(shape, lambda b, i: (0,) * len(shape))
    aug = jax.ShapeDtypeStruct((B, N_FOX_HEADS, S, LANES), BF16)
    aug_spec = pl.BlockSpec((1, N_FOX_HEADS, tm, LANES), lambda b, i: (b, 0, i, 0))
    return pl.pallas_call(
        _inproj_kernel,
        out_shape=(jax.ShapeDtypeStruct((B, S, 4 * D_RET), BF16),
                   jax.ShapeDtypeStruct((B, S, D_GM), BF16),
                   aug, aug,
                   jax.ShapeDtypeStruct((B, N_FOX_HEADS, S // FOX_TK, VT_ROWS, FOX_TK), BF16)),
        grid=(B, S // tm),
        in_specs=[pl.BlockSpec((1, tm, D), lambda b, i: (b, i, 0)),
                  pl.BlockSpec((1, tm, D), lambda b, i: (b, jnp.minimum(i + 1, n_s - 1), 0)),
                  pl.BlockSpec((1, 1, D), lambda b, i: (b, 0, 0)),
                  pl.BlockSpec((1, 1, D), lambda b, i: (b, 0, 0)),
                  const((1, D)),
                  _layer_resident(w_in, layer),
                  pl.BlockSpec((tm, LANES), lambda b, i: (i, 0)),
                  pl.BlockSpec((tm, LANES), lambda b, i: (i, 0)),
                  const((CHUNK, D_RET)), const((CHUNK, D_RET)),
                  const((1, D_GM)), const((1, D_GM)),
                  const((CHUNK, N_GM_HEADS * CHUNK)), const((CHUNK, D_GM)),
                  const((1, LANES)), const((1, D_FOX)), const((1, D_FOX))],
        out_specs=(pl.BlockSpec((1, tm, 4 * D_RET), lambda b, i: (b, i, 0)),
                   pl.BlockSpec((1, tm, D_GM), lambda b, i: (b, i, 0)),
                   aug_spec, aug_spec,
                   pl.BlockSpec((1, N_FOX_HEADS, 1, VT_ROWS, tm),
                                lambda b, i: (b, 0, i // per_kb, 0, i % per_kb))),
        scratch_shapes=[pltpu.VMEM((tm, D), BF16), pltpu.VMEM((tm, D), BF16),
                        pltpu.VMEM((tm, LANES), F32), pltpu.VMEM((tm, LANES), F32),
                        pltpu.VMEM((SUBLANES, LANES), F32)],
        compiler_params=_cparams("arbitrary", "arbitrary"),
        name="in_proj",
    )(x, x, sc, sh, g, w_in, cos_t, sin_t, dq, dk, ln_g, ln_b, ws_cat, gate_bias, bf_row, qg, kg)


def _ret_kernel(q_ref, k_ref, v_ref, g_ref, ng_ref, wo_ref, wu_ref, wd_ref,
                o_ref, wo_out, wu_out, wd_out, state_ref):
    wo_out[...] = wo_ref[0].astype(BF16)
    wu_out[...] = wu_ref[0].astype(BF16)
    wd_out[...] = wd_ref[0].astype(BF16)

    tr = q_ref.shape[1]
    n_pairs = N_RET_HEADS // 2
    low = _lane_iota((CHUNK, LANES)) < HEAD_DIM
    keep0 = jnp.where(low, 1.0, 0.0).astype(BF16)
    keep1 = jnp.where(low, 0.0, 1.0).astype(BF16)
    causal = (lax.broadcasted_iota(jnp.int32, (CHUNK, CHUNK), 0) >=
              lax.broadcasted_iota(jnp.int32, (CHUNK, CHUNK), 1))
    blockdiag = (lax.broadcasted_iota(jnp.int32, (LANES, LANES), 0) < HEAD_DIM) == low
    dn_t = (((1,), (1,)), ((), ()))
    dn_ta = (((0,), (0,)), ((), ()))

    @pl.when(pl.program_id(1) == 0)
    def _():
        state_ref[...] = jnp.zeros_like(state_ref)

    n_c = tr // CHUNK
    tiles = [(p, c) for p in range(n_pairs) for c in range(n_c)]
    sl = lambda p, c: (slice(c * CHUNK, (c + 1) * CHUNK), slice(p * LANES, (p + 1) * LANES))
    kvs, scores = {}, {}
    for p, c in tiles:
        rs, cs = sl(p, c)
        q, k, v = q_ref[0, rs, cs], k_ref[0, rs, cs], v_ref[0, rs, cs]
        kvs[p, c] = lax.dot_general(k, v, dn_ta, preferred_element_type=F32)
        for hh, keep in enumerate((keep0, keep1)):
            s = lax.dot_general(q * keep, k, dn_t, preferred_element_type=F32)
            scores[p, c, hh] = jnp.where(causal, s, 0.0).astype(BF16)
    states = {}
    for p in range(n_pairs):
        chunk_decay = jnp.where(low[0:1, :], math.exp(_ret_log_gamma(2 * p) * CHUNK),
                                math.exp(_ret_log_gamma(2 * p + 1) * CHUNK))
        state = state_ref[p]
        for c in range(n_c):
            states[p, c] = state.astype(BF16)
            state = (state + jnp.where(blockdiag, kvs[p, c], 0.0)) * chunk_decay
        state_ref[p] = state
    for p, c in tiles:
        rs, cs = sl(p, c)
        q = q_ref[0, rs, cs]
        rhs = jnp.concatenate([v_ref[0, rs, cs], states[p, c]], axis=0)
        ys = [jnp.dot(jnp.concatenate([scores[p, c, hh], q * keep], axis=1), rhs,
                      preferred_element_type=F32) for hh, keep in enumerate((keep0, keep1))]
        y = jnp.where(low, ys[0], ys[1])
        yn = y * _pair_rms_scale(y) * ng_ref[:, cs]
        o_ref[0, rs, cs] = (yn * g_ref[0, rs, cs].astype(F32)).astype(BF16)


def _retention(p_ret, ng, layer, w_o, w_up, w_down, tr):
    B, S, _ = p_ret.shape
    n_s = S // tr
    steps = B * n_s
    col = lambda j: pl.BlockSpec((1, tr, D_RET), lambda b, i, j=j: (b, i, j))

    def slab(w, n_slabs):
        rows = w.shape[1] // n_slabs
        assert rows * n_slabs == w.shape[1] and rows % 16 == 0 and steps % n_slabs == 0
        rep = steps // n_slabs
        return pl.BlockSpec((1, rows) + w.shape[2:],
                            lambda b, i: (layer, (b * n_s + i) // rep, 0))

    def slab_out(w, n_slabs):
        rows = w.shape[1] // n_slabs
        rep = steps // n_slabs
        return pl.BlockSpec((rows,) + w.shape[2:], lambda b, i: ((b * n_s + i) // rep, 0))

    slabs = ((w_o, steps), (w_up, steps), (w_down, steps // 2))
    return pl.pallas_call(
        _ret_kernel,
        out_shape=(jax.ShapeDtypeStruct((B, S, D_RET), BF16),)
        + tuple(jax.ShapeDtypeStruct(w.shape[1:], BF16) for w, _ in slabs),
        grid=(B, n_s),
        in_specs=[col(0), col(1), col(2), col(3),
                  pl.BlockSpec((1, D_RET), lambda b, i: (0, 0))]
        + [slab(w, n) for w, n in slabs],
        out_specs=(pl.BlockSpec((1, tr, D_RET), lambda b, i: (b, i, 0)),)
        + tuple(slab_out(w, n) for w, n in slabs),
        scratch_shapes=[pltpu.VMEM((N_RET_HEADS // 2, LANES, LANES), F32)],
        compiler_params=_cparams("arbitrary", "arbitrary"),
        name="retention",
    )(p_ret, p_ret, p_ret, p_ret, ng, w_o, w_up, w_down)


def _fox_kernel(q_ref, k_ref, vt_ref, o_ref, m_ref, acc_ref, s0_ref, s1_ref, mb0_ref, mb1_ref):
    s_refs = (s0_ref, s1_ref)
    mb_refs = (mb0_ref, mb1_ref)
    i = pl.program_id(2)
    n_qt = FOX_BQ // FOX_TQ
    units = [(hh, qt) for hh in range(2) for qt in range(n_qt)]
    n_u = len(units)
    dn_t = (((1,), (1,)), ((), ()))
    m_ref[...] = jnp.full(m_ref.shape, NEG_INF, F32)
    acc_ref[...] = jnp.zeros_like(acc_ref)

    def diag_keys(qt, half):
        d = qt * FOX_TQ - half * FOX_TK
        nk = min(max(d + FOX_TQ, 0), FOX_TK)
        return nk, (d if nk - 1 > d else None)

    def scores(j, slot, u):
        hh, qt = units[u]
        start = pl.multiple_of(j * FOX_TK, FOX_TK)
        k = k_ref[0, hh, pl.ds(start, FOX_TK), :]
        q = q_ref[0, hh, qt * FOX_TQ:(qt + 1) * FOX_TQ, :]
        s = lax.dot_general(k, q, dn_t, preferred_element_type=F32)
        s_refs[slot][u] = s
        mb_refs[slot][u] = jnp.max(s, axis=0, keepdims=True)

    def update(j, slot, u, diag_half=None):
        hh, qt = units[u]
        nk, mask_off = (FOX_TK, None) if diag_half is None else diag_keys(qt, diag_half)
        if nk == 0:
            return
        s = s_refs[slot][u, 0:nk, :]
        m_blk = mb_refs[slot][u]
        if mask_off is not None or nk < FOX_TK:
            if mask_off is not None:
                key = lax.broadcasted_iota(jnp.int32, s.shape, 0)
                qry = lax.broadcasted_iota(jnp.int32, s.shape, 1)
                s = jnp.where(key <= qry + mask_off, s, NEG_INF)
            m_blk = jnp.max(s, axis=0, keepdims=True)
        m_old = m_ref[hh, qt]
        m_new = jnp.maximum(m_old, m_blk)
        p = jnp.exp2((s - m_new).astype(BF16))
        vt = vt_ref[0, hh, j, :, 0:nk]
        acc_ref[hh, qt] = jnp.exp2(m_old - m_new) * acc_ref[hh, qt] + jnp.dot(
            vt, p, preferred_element_type=F32)
        m_ref[hh, qt] = m_new

    for u in range(n_u):
        scores(0, 0, u)

    def body(j, carry):
        for u in range(n_u):
            scores(2 * j + 1, 1, u)
            update(2 * j, 0, u)
        for u in range(n_u):
            scores(2 * j + 2, 0, u)
            update(2 * j + 1, 1, u)
        return carry

    lax.fori_loop(0, i, body, 0)
    for u in range(n_u):
        if diag_keys(units[u][1], 1)[0] > 0:
            scores(2 * i + 1, 1, u)
        update(2 * i, 0, u, 0)
    for u in range(n_u):
        update(2 * i + 1, 1, u, 1)

    for qt in range(n_qt):
        rows = []
        for hh in range(2):
            acc = acc_ref[hh, qt]
            rows.append(acc[0:HEAD_DIM] * (1.0 / acc[HEAD_DIM:HEAD_DIM + 1]))
        pair_t = jnp.concatenate(rows, axis=0)
        o_ref[0, qt * FOX_TQ:(qt + 1) * FOX_TQ, :] = pair_t.T.astype(BF16)


def _fox_attention(qa, ka, vt):
    B, H, S, _ = qa.shape
    n_qt = FOX_BQ // FOX_TQ
    stage = pltpu.VMEM((2 * n_qt, FOX_TK, FOX_TQ), F32)
    stage_max = pltpu.VMEM((2 * n_qt, 1, FOX_TQ), F32)
    return pl.pallas_call(
        _fox_kernel,
        out_shape=jax.ShapeDtypeStruct((B, S, D_FOX), BF16),
        grid=(B, H // 2, S // FOX_BQ),
        in_specs=[pl.BlockSpec((1, 2, FOX_BQ, LANES), lambda b, p, i: (b, p, i, 0)),
                  pl.BlockSpec((1, 2, S, LANES), lambda b, p, i: (b, p, 0, 0)),
                  pl.BlockSpec((1, 2, S // FOX_TK, VT_ROWS, FOX_TK),
                               lambda b, p, i: (b, p, 0, 0, 0))],
        out_specs=pl.BlockSpec((1, FOX_BQ, LANES), lambda b, p, i: (b, i, p)),
        scratch_shapes=[pltpu.VMEM((2, n_qt, 1, FOX_TQ), F32),
                        pltpu.VMEM((2, n_qt, VT_ROWS, FOX_TQ), F32),
                        stage, stage, stage_max, stage_max],
        compiler_params=_cparams("arbitrary", "arbitrary", "arbitrary"),
        name="fox_attention",
    )(qa, ka, vt)


FF_CHUNK = MXU_DIM


def _mix_ffn_kernel(yr_ref, yg_ref, yf_ref, x_ref, gt1_ref, sc_ref, sh_ref, gt2_ref, g_ref,
                    wo_ref, wu_ref, cw_ref, cb_ref, wd_ref, o_ref, carry_ref, stage_ref, act_ref):
    tm = x_ref.shape[1]

    @pl.when(pl.program_id(1) == 0)
    def _():
        carry_ref[...] = jnp.zeros_like(carry_ref)

    halves = []
    for rs in (slice(0, tm // 2), slice(tm // 2, tm)):
        mix = jnp.concatenate([yr_ref[0, rs], yg_ref[0, rs], yf_ref[0, rs]], axis=-1)
        xn = x_ref[0, rs] + gt1_ref[0] * jnp.dot(mix, wo_ref[...], preferred_element_type=F32)
        o_ref[0, rs] = xn
        halves.append(xn)
    h = jnp.concatenate([_mod_rms(xn, g_ref[...], sc_ref[0], sh_ref[0]).astype(BF16)
                         for xn in halves], axis=0)

    def conv_cols(c0):
        cs = slice(c0, c0 + FF_CHUNK)
        up = jnp.dot(h, wu_ref[:, cs], preferred_element_type=F32)
        stage_ref[0:SUBLANES, :] = carry_ref[:, cs]
        stage_ref[SUBLANES:, :] = up
        carry_ref[:, cs] = up[tm - SUBLANES:, :]
        prev1 = stage_ref[SUBLANES - 1:SUBLANES - 1 + tm, :]
        prev2 = stage_ref[SUBLANES - 2:SUBLANES - 2 + tm, :]
        return (cw_ref[0, 2:3, cs] * up + cw_ref[0, 1:2, cs] * prev1 + cw_ref[0, 0:1, cs] * prev2
                + cb_ref[0, :, cs])

    for f in range(D_FF // FF_CHUNK):
        a = conv_cols(f * FF_CHUNK)
        b = conv_cols(D_FF + f * FF_CHUNK)
        act_ref[:, f * FF_CHUNK:(f + 1) * FF_CHUNK] = ((a * jax.nn.sigmoid(a)) * b).astype(BF16)
    down = jnp.dot(act_ref[...], wd_ref[...], preferred_element_type=F32)
    o_ref[0] = o_ref[0] + gt2_ref[0] * down


def _mix_ffn(yr, yg, yf, x, gt1, sc, sh, gt2, g, layer, w_o, w_up, conv_w, conv_b, w_down, tm):
    B, S, D = x.shape
    row = lambda n: pl.BlockSpec((1, tm, n), lambda b, i: (b, i, 0))
    vec = pl.BlockSpec((1, 1, D), lambda b, i: (b, 0, 0))
    return pl.pallas_call(
        _mix_ffn_kernel,
        out_shape=jax.ShapeDtypeStruct((B, S, D), F32),
        grid=(B, S // tm),
        in_specs=[row(D_RET), row(D_GM), row(D_FOX), row(D), vec, vec, vec, vec,
                  pl.BlockSpec((1, D), lambda b, i: (0, 0)),
                  _resident(w_o), _resident(w_up),
                  _layer_resident(conv_w, layer), _layer_resident(conv_b, layer),
                  _resident(w_down)],
        out_specs=row(D),
        scratch_shapes=[pltpu.VMEM((SUBLANES, 2 * D_FF), F32),
                        pltpu.VMEM((tm + SUBLANES, FF_CHUNK), F32),
                        pltpu.VMEM((tm, D_FF), BF16)],
        compiler_params=_cparams("arbitrary", "arbitrary"),
        name="mix_ffn",
    )(yr, yg, yf, x, gt1, sc, sh, gt2, g, w_o, w_up, conv_w, conv_b, w_down)


def kernel(x, c, ada_w, ada_b, norm1_g, w_in, ret_norm_g, gm_ln_g, gm_ln_b, gm_ws, gm_bs,
           fox_qn_g, fox_kn_g, fox_bf, w_o, norm2_g, w_up, conv_w, conv_b, w_down):
    B, S, D = x.shape
    L = ada_w.shape[0]
    assert D == D_MODEL and S % FOX_BQ == 0
    tm = 512
    tm_ffn = min(S, 2 * tm)

    mod = _modulation(c, ada_w, ada_b).reshape(L, B, 6, 1, D)
    cos_t, sin_t, dq, dk = _rope_tables(S)
    w_in_b = w_in.astype(BF16)
    conv_b3 = conv_b.reshape(L, 1, 2 * D_FF)

    for l in range(L):
        sh1, sc1, gt1, sh2, sc2, gt2 = (mod[l, :, i] for i in range(6))
        ws_cat = gm_ws[l].transpose(1, 0, 2).reshape(CHUNK, N_GM_HEADS * CHUNK)
        gate_bias = jnp.repeat(gm_bs[l].T, HEAD_DIM, axis=1)
        bf_row = jnp.pad(fox_bf[l], (FL_LANE0, 0)).reshape(1, LANES)
        p_ret, y_g, qa, ka, vt = _in_projection(
            x, sc1, sh1, norm1_g[l].reshape(1, D), w_in_b, l, cos_t, sin_t, dq, dk,
            gm_ln_g[l].reshape(1, D_GM), gm_ln_b[l].reshape(1, D_GM), ws_cat, gate_bias, bf_row,
            jnp.tile(fox_qn_g[l], N_FOX_HEADS).reshape(1, D_FOX),
            jnp.tile(fox_kn_g[l], N_FOX_HEADS).reshape(1, D_FOX), tm)

        y_r, w_o_b, w_up_b, w_down_b = _retention(p_ret, ret_norm_g[l].reshape(1, D_RET), l,
                                                   w_o, w_up, w_down, tm)
        y_f = _fox_attention(qa, ka, vt)

        x = _mix_ffn(y_r, y_g, y_f, x, gt1, sc2, sh2, gt2, norm2_g[l].reshape(1, D), l,
                     w_o_b, w_up_b, conv_w, conv_b3, w_down_b, tm_ffn)
    return x
```

```python
import functools
import math

import jax
import jax.numpy as jnp
from jax import lax
from jax.experimental import pallas as pl
from jax.experimental.pallas import tpu as pltpu

D_MODEL = 1024
HEAD_DIM = 64
N_RET_HEADS = 6
N_GM_HEADS = 4
N_FOX_HEADS = 6
D_RET = N_RET_HEADS * HEAD_DIM
D_GM = N_GM_HEADS * HEAD_DIM
D_FOX = N_FOX_HEADS * HEAD_DIM
CHUNK = 128
D_FF = 2816
ROPE_BASE = 10000.0
EPS = 1e-6
NEG_INF = -1e30

LANES = 128
SUBLANES = 8
MXU_DIM = 256
VMEM_LIMIT = 52 * 1024 * 1024

N_MAIN = 4 * D_RET + 2 * D_GM + 3 * D_FOX
GM_COL0 = 4 * D_RET
FOX_COL0 = GM_COL0 + 2 * D_GM
N_IN = N_MAIN + N_FOX_HEADS
FL_LANE0 = LANES - N_FOX_HEADS

LOG2E = math.log2(math.e)
FOX_TK = 1024
FOX_TQ = MXU_DIM
FOX_BQ = 2 * FOX_TK
VT_ROWS = 80

F32 = jnp.float32
BF16 = jnp.bfloat16


def _cparams(*sem):
    return pltpu.CompilerParams(dimension_semantics=sem, vmem_limit_bytes=VMEM_LIMIT)


def _resident(whole):
    nd = whole.ndim
    return pl.BlockSpec(whole.shape, lambda *_: (0,) * nd, pipeline_mode=pl.Buffered(1))


def _layer_resident(stacked, layer):
    nd = stacked.ndim - 1
    return pl.BlockSpec((1,) + stacked.shape[1:], lambda *_: (layer,) + (0,) * nd,
                        pipeline_mode=pl.Buffered(1))


def _lane_iota(shape):
    return lax.broadcasted_iota(jnp.int32, shape, len(shape) - 1)


def _mod_kernel(ct_ref, w_ref, b_ref, o_ref):
    ct = ct_ref[...]
    cond_t = ct * jax.nn.sigmoid(ct)
    w = w_ref[0]
    rows = [jnp.sum(cond_t[:, b:b + 1] * w, axis=0, keepdims=True)
            for b in range(ct.shape[1])]
    o_ref[0] = jnp.concatenate(rows, axis=0) + b_ref[0]


def _modulation(c, ada_w, ada_b):
    L, D, N = ada_w.shape
    B = c.shape[0]
    tn = 1536
    return pl.pallas_call(
        _mod_kernel,
        out_shape=jax.ShapeDtypeStruct((L, B, N), F32),
        grid=(L, N // tn),
        in_specs=[pl.BlockSpec((D, B), lambda l, j: (0, 0)),
                  pl.BlockSpec((1, D, tn), lambda l, j: (l, 0, j)),
                  pl.BlockSpec((1, 1, tn), lambda l, j: (l, 0, j))],
        out_specs=pl.BlockSpec((1, B, tn), lambda l, j: (l, 0, j)),
        compiler_params=_cparams("arbitrary", "arbitrary"),
        name="adaln_mod",
    )(c.T, ada_w, ada_b.reshape(L, 1, N))


def _ret_log_gamma(h):
    return math.log(1.0 - 2.0 ** (-5.0 - h))


def _rope_kernel(inv_ref, cos_ref, sin_ref, dq_ref, dk_ref):
    ts = cos_ref.shape[0]
    pos = (lax.broadcasted_iota(jnp.int32, (ts, LANES), 0) + pl.program_id(0) * ts).astype(F32)
    ang = pos * inv_ref[...]
    first_half = (_lane_iota((ts, LANES)) % HEAD_DIM) < (HEAD_DIM // 2)
    cos_ref[...] = jnp.cos(ang)
    s = jnp.sin(ang)
    sin_ref[...] = jnp.where(first_half, -s, s)

    head = _lane_iota((CHUNK, D_RET)) // HEAD_DIM
    lg = jnp.zeros((CHUNK, D_RET), F32)
    for h in range(N_RET_HEADS):
        lg = jnp.where(head == h, _ret_log_gamma(h), lg)
    t1 = lax.broadcasted_iota(jnp.int32, (CHUNK, D_RET), 0).astype(F32) + 1.0
    dq_ref[...] = jnp.exp(lg * t1)
    dk_ref[...] = jnp.exp(-lg * t1) * (HEAD_DIM ** -0.5)


def _rope_tables(S):
    half = HEAD_DIM // 2
    inv = ROPE_BASE ** (-jnp.arange(half, dtype=F32) / half)
    inv_row = jnp.tile(inv, LANES // half).reshape(1, LANES)
    ts = 512
    decay = jax.ShapeDtypeStruct((CHUNK, D_RET), F32)
    decay_spec = pl.BlockSpec((CHUNK, D_RET), lambda i: (0, 0))
    return pl.pallas_call(
        _rope_kernel,
        out_shape=(jax.ShapeDtypeStruct((S, LANES), F32),) * 2 + (decay, decay),
        grid=(S // ts,),
        in_specs=[pl.BlockSpec((1, LANES), lambda i: (0, 0))],
        out_specs=(pl.BlockSpec((ts, LANES), lambda i: (i, 0)),) * 2 + (decay_spec, decay_spec),
        compiler_params=_cparams("arbitrary"),
        name="rope_tables",
    )(inv_row)


def _mod_rms(x, g, sc, sh):
    ms = jnp.mean(x * x, axis=-1, keepdims=True)
    return (x * lax.rsqrt(ms + EPS) * g) * (1.0 + sc) + sh


def _swap_half_heads(x):
    first_half = (_lane_iota(x.shape) % HEAD_DIM) < (HEAD_DIM // 2)
    return jnp.where(first_half, pltpu.roll(x, LANES - HEAD_DIM // 2, 1),
                     pltpu.roll(x, HEAD_DIM // 2, 1))


def _gelu(x):
    return 0.5 * x * (1.0 + jnp.tanh(math.sqrt(2.0 / math.pi) * (x + 0.044715 * (x * x * x))))


def _pair_rms_scale(x):
    low = _lane_iota(x.shape) < HEAD_DIM
    sq = x * x
    ss0 = jnp.sum(jnp.where(low, sq, 0.0), axis=-1, keepdims=True)
    ss1 = jnp.sum(jnp.where(low, 0.0, sq), axis=-1, keepdims=True)
    return jnp.where(low, lax.rsqrt(ss0 * (1.0 / HEAD_DIM) + EPS),
                     lax.rsqrt(ss1 * (1.0 / HEAD_DIM) + EPS))


def _inproj_kernel(x_ref, xn_ref, sc_ref, sh_ref, g_ref, w_ref, cos_ref, sin_ref, dq_ref, dk_ref,
                   lng_ref, lnb_ref, ws_ref, gb_ref, bf_ref, qg_ref, kg_ref,
                   pret_ref, yg_ref, qa_ref, ka_ref, vt_ref,
                   h_a, h_b, nc_a, nc_b, carry_ref):
    tm = x_ref.shape[1]
    step = pl.program_id(1)

    def prepare(x_tile_ref, h_out, nc_out):
        stages, box = [], {}

        def norm_rows(q):
            rs = slice(q * CHUNK, (q + 1) * CHUNK)
            h_out[rs, :] = _mod_rms(x_tile_ref[0, rs], g_ref[...], sc_ref[0], sh_ref[0]).astype(BF16)

        def forget_logits():
            z = jnp.dot(h_out[...], w_ref[0, :, N_IN - LANES:N_IN],
                        preferred_element_type=F32) + bf_ref[...]
            box["logf"] = jnp.minimum(z, 0.0) - jnp.log1p(jnp.exp(-jnp.abs(z)))

        def cumulate():
            tri = (lax.broadcasted_iota(jnp.int32, (CHUNK, CHUNK), 1) <=
                   lax.broadcasted_iota(jnp.int32, (CHUNK, CHUNK), 0)).astype(F32)
            logf = box["logf"]
            within = [jnp.dot(tri, logf[c * CHUNK:(c + 1) * CHUNK],
                              precision=lax.Precision.HIGHEST, preferred_element_type=F32)
                      for c in range(tm // CHUNK)]
            carry = carry_ref[0:1, :]
            for c, cum in enumerate(within):
                nc_out[c * CHUNK:(c + 1) * CHUNK, :] = (cum + carry) * (-LOG2E)
                carry = carry + cum[CHUNK - 1:CHUNK, :]
            carry_ref[...] = jnp.broadcast_to(carry, carry_ref.shape)

        for q in range(tm // CHUNK):
            stages.append(functools.partial(norm_rows, q))
        return stages + [forget_logits, cumulate]

    @pl.when(step == 0)
    def _():
        carry_ref[...] = jnp.zeros_like(carry_ref)
        for stage in prepare(x_ref, h_a, nc_a):
            stage()

    @pl.when(step % 2 == 0)
    def _():
        _inproj_tile(h_a, nc_a, prepare(xn_ref, h_b, nc_b), w_ref, cos_ref, sin_ref,
                     dq_ref, dk_ref, lng_ref, lnb_ref, ws_ref, gb_ref, qg_ref, kg_ref,
                     pret_ref, yg_ref, qa_ref, ka_ref, vt_ref)

    @pl.when(step % 2 == 1)
    def _():
        _inproj_tile(h_b, nc_b, prepare(xn_ref, h_a, nc_a), w_ref, cos_ref, sin_ref,
                     dq_ref, dk_ref, lng_ref, lnb_ref, ws_ref, gb_ref, qg_ref, kg_ref,
                     pret_ref, yg_ref, qa_ref, ka_ref, vt_ref)


def _inproj_tile(h_ref, nc_ref, next_stages, w_ref, cos_ref, sin_ref, dq_ref, dk_ref, lng_ref,
                 lnb_ref, ws_ref, gb_ref, qg_ref, kg_ref, pret_ref, yg_ref, qa_ref, ka_ref, vt_ref):
    tm = h_ref.shape[0]
    h = h_ref[...]
    neg_cum = nc_ref[...]
    bias_hi = neg_cum.astype(BF16).astype(F32)
    rest = neg_cum - bias_hi
    bias_mid = rest.astype(BF16).astype(F32)
    bias_lo = (rest - bias_mid).astype(BF16).astype(F32)

    lane = _lane_iota((tm, LANES))
    low = lane < HEAD_DIM
    ones_row = jnp.where(lax.broadcasted_iota(jnp.int32, (VT_ROWS - HEAD_DIM, tm), 0) == 0,
                         1.0, 0.0).astype(BF16)

    def fox_tiles(piece, kind, pair):
        if kind == 2:
            v_t = piece.T.astype(BF16)
        else:
            cols = slice(pair * LANES, (pair + 1) * LANES)
            gain = qg_ref[:, cols] * (HEAD_DIM ** -0.5 * LOG2E) if kind == 0 else kg_ref[:, cols]
            normed = piece * _pair_rms_scale(piece) * gain
        for half in range(2):
            hd = 2 * pair + half
            own = low if half == 0 else jnp.logical_not(low)
            b0 = HEAD_DIM * (1 - half)
            if kind == 0:
                ones = jnp.where((lane >= b0) & (lane < b0 + 3), 1.0, 0.0)
                qa_ref[0, hd] = jnp.where(own, normed, ones).astype(BF16)
            elif kind == 1:
                col = slice(FL_LANE0 + hd, FL_LANE0 + hd + 1)
                bias = jnp.where(lane == b0, bias_hi[:, col],
                                 jnp.where(lane == b0 + 1, bias_mid[:, col],
                                           jnp.where(lane == b0 + 2, bias_lo[:, col], 0.0)))
                ka_ref[0, hd] = jnp.where(own, normed, bias).astype(BF16)
            else:
                vt_ref[0, hd, 0, 0:HEAD_DIM, :] = v_t[half * HEAD_DIM:(half + 1) * HEAD_DIM]
                vt_ref[0, hd, 0, HEAD_DIM:VT_ROWS, :] = ones_row

    def spatial_gate(u, vn):
        lane_c = _lane_iota((CHUNK, LANES))
        keep_a = jnp.where(lane_c < HEAD_DIM, 1.0, 0.0).astype(BF16)
        keep_b = jnp.where(lane_c < HEAD_DIM, 0.0, 1.0).astype(BF16)
        row = lax.broadcasted_iota(jnp.int32, (CHUNK, 2 * CHUNK), 0)
        col = _lane_iota((CHUNK, 2 * CHUNK)) % CHUNK
        vb = vn.astype(BF16)
        for pair in range(N_GM_HEADS // 2):
            w_pair = jnp.where(col <= row, ws_ref[:, pair * 2 * CHUNK:(pair + 1) * 2 * CHUNK],
                               0.0).astype(BF16)
            ls = slice(pair * LANES, (pair + 1) * LANES)
            for c in range(tm // CHUNK):
                rs = slice(c * CHUNK, (c + 1) * CHUNK)
                v_tile = vb[rs, ls]
                stacked = jnp.concatenate([v_tile * keep_a, v_tile * keep_b], axis=0)
                mixed = jnp.dot(w_pair, stacked, preferred_element_type=F32)
                yg_ref[0, rs, ls] = (u[rs, ls] * (mixed + gb_ref[:, ls])).astype(BF16)

    def rotary_decay(piece, table_ref, tile):
        rot = piece * cos_ref[...] + _swap_half_heads(piece) * sin_ref[...]
        table = table_ref[:, tile * LANES:(tile + 1) * LANES]
        return rot * jnp.concatenate([table] * (tm // CHUNK), axis=0)

    for lo in range(0, N_MAIN, MXU_DIM):
        width = min(MXU_DIM, N_MAIN - lo)
        r = jnp.dot(h, w_ref[0, :, lo:lo + width], preferred_element_type=F32)
        if next_stages:
            next_stages.pop(0)()
        if lo == GM_COL0:
            gate_u = _gelu(r)
            continue
        if lo == GM_COL0 + D_GM:
            v = _gelu(r)
            mu = jnp.mean(v, axis=-1, keepdims=True)
            vc = v - mu
            var = jnp.mean(vc * vc, axis=-1, keepdims=True)
            spatial_gate(gate_u, vc * lax.rsqrt(var + EPS) * lng_ref[...] + lnb_ref[...])
            continue
        for off in range(0, width, LANES):
            c0 = lo + off
            piece = r[:, off:off + LANES]
            if c0 < D_RET:
                piece = rotary_decay(piece, dq_ref, c0 // LANES)
            elif c0 < 2 * D_RET:
                piece = rotary_decay(piece, dk_ref, (c0 - D_RET) // LANES)
            elif 3 * D_RET <= c0 < 4 * D_RET:
                piece = piece * jax.nn.sigmoid(piece)
            if c0 < GM_COL0:
                pret_ref[0, :, c0:c0 + LANES] = piece.astype(BF16)
            else:
                kind, pair = divmod((c0 - FOX_COL0) // LANES, N_FOX_HEADS // 2)
                fox_tiles(piece, kind, pair)


def _in_projection(x, sc, sh, g, w_in, layer, cos_t, sin_t, dq, dk, ln_g, ln_b, ws_cat,
                   gate_bias, bf_row, qg, kg, tm):
    B, S, D = x.shape
    assert FOX_TK % tm == 0 and (S // tm) % 2 == 0
    per_kb = FOX_TK // tm
    n_s = S // tm
    const = lambda shape: pl.BlockSpec(shape, lambda b, i: (0,) * len(shape))
    aug = jax.ShapeDtypeStruct((B, N_FOX_HEADS, S, LANES), BF16)
    aug_spec = pl.BlockSpec((1, N_FOX_HEADS, tm, LANES), lambda b, i: (b, 0, i, 0))
    return pl.pallas_call(
        _inproj_kernel,
        out_shape=(jax.ShapeDtypeStruct((B, S, 4 * D_RET), BF16),
                   jax.ShapeDtypeStruct((B, S, D_GM), BF16),
                   aug, aug,
                   jax.ShapeDtypeStruct((B, N_FOX_HEADS, S // FOX_TK, VT_ROWS, FOX_TK), BF16)),
        grid=(B, S // tm),
        in_specs=[pl.BlockSpec((1, tm, D), lambda b, i: (b, i, 0)),
                  pl.BlockSpec((1, tm, D), lambda b, i: (b, jnp.minimum(i + 1, n_s - 1), 0)),
                  pl.BlockSpec((1, 1, D), lambda b, i: (b, 0, 0)),
                  pl.BlockSpec((1, 1, D), lambda b, i: (b, 0, 0)),
                  const((1, D)),
                  _layer_resident(w_in, layer),
                  pl.BlockSpec((tm, LANES), lambda b, i: (i, 0)),
                  pl.BlockSpec((tm, LANES), lambda b, i: (i, 0)),
                  const((CHUNK, D_RET)), const((CHUNK, D_RET)),
                  const((1, D_GM)), const((1, D_GM)),
                  const((CHUNK, N_GM_HEADS * CHUNK)), const((CHUNK, D_GM)),
                  const((1, LANES)), const((1, D_FOX)), const((1, D_FOX))],
        out_specs=(pl.BlockSpec((1, tm, 4 * D_RET), lambda b, i: (b, i, 0)),
                   pl.BlockSpec((1, tm, D_GM), lambda b, i: (b, i, 0)),
                   aug_spec, aug_spec,
                   pl.BlockSpec((1, N_FOX_HEADS, 1, VT_ROWS, tm),
                                lambda b, i: (b, 0, i // per_kb, 0, i % per_kb))),
        scratch_shapes=[pltpu.VMEM((tm, D), BF16), pltpu.VMEM((tm, D), BF16),
                        pltpu.VMEM((tm, LANES), F32), pltpu.VMEM((tm, LANES), F32),
                        pltpu.VMEM((SUBLANES, LANES), F32)],
        compiler_params=_cparams("arbitrary", "arbitrary"),
        name="in_proj",
    )(x, x, sc, sh, g, w_in, cos_t, sin_t, dq, dk, ln_g, ln_b, ws_cat, gate_bias, bf_row, qg, kg)


def _ret_kernel(q_ref, k_ref, v_ref, g_ref, ng_ref, wo_ref, wu_ref, wd_ref,
                o_ref, wo_out, wu_out, wd_out, state_ref):
    wo_out[...] = wo_ref[0].astype(BF16)
    wu_out[...] = wu_ref[0].astype(BF16)
    wd_out[...] = wd_ref[0].astype(BF16)

    tr = q_ref.shape[1]
    n_pairs = N_RET_HEADS // 2
    low = _lane_iota((CHUNK, LANES)) < HEAD_DIM
    keep0 = jnp.where(low, 1.0, 0.0).astype(BF16)
    keep1 = jnp.where(low, 0.0, 1.0).astype(BF16)
    causal = (lax.broadcasted_iota(jnp.int32, (CHUNK, CHUNK), 0) >=
              lax.broadcasted_iota(jnp.int32, (CHUNK, CHUNK), 1))
    blockdiag = (lax.broadcasted_iota(jnp.int32, (LANES, LANES), 0) < HEAD_DIM) == low
    dn_t = (((1,), (1,)), ((), ()))
    dn_ta = (((0,), (0,)), ((), ()))

    @pl.when(pl.program_id(1) == 0)
    def _():
        state_ref[...] = jnp.zeros_like(state_ref)

    n_c = tr // CHUNK
    tiles = [(p, c) for p in range(n_pairs) for c in range(n_c)]
    sl = lambda p, c: (slice(c * CHUNK, (c + 1) * CHUNK), slice(p * LANES, (p + 1) * LANES))
    kvs, scores = {}, {}
    for p, c in tiles:
        rs, cs = sl(p, c)
        q, k, v = q_ref[0, rs, cs], k_ref[0, rs, cs], v_ref[0, rs, cs]
        kvs[p, c] = lax.dot_general(k, v, dn_ta, preferred_element_type=F32)
        for hh, keep in enumerate((keep0, keep1)):
            s = lax.dot_general(q * keep, k, dn_t, preferred_element_type=F32)
            scores[p, c, hh] = jnp.where(causal, s, 0.0).astype(BF16)
    states = {}
    for p in range(n_pairs):
        chunk_decay = jnp.where(low[0:1, :], math.exp(_ret_log_gamma(2 * p) * CHUNK),
                                math.exp(_ret_log_gamma(2 * p + 1) * CHUNK))
        state = state_ref[p]
        for c in range(n_c):
            states[p, c] = state.astype(BF16)
            state = (state + jnp.where(blockdiag, kvs[p, c], 0.0)) * chunk_decay
        state_ref[p] = state
    for p, c in tiles:
        rs, cs = sl(p, c)
        q = q_ref[0, rs, cs]
        rhs = jnp.concatenate([v_ref[0, rs, cs], states[p, c]], axis=0)
        ys = [jnp.dot(jnp.concatenate([scores[p, c, hh], q * keep], axis=1), rhs,
                      preferred_element_type=F32) for hh, keep in enumerate((keep0, keep1))]
        y = jnp.where(low, ys[0], ys[1])
        yn = y * _pair_rms_scale(y) * ng_ref[:, cs]
        o_ref[0, rs, cs] = (yn * g_ref[0, rs, cs].astype(F32)).astype(BF16)


def _retention(p_ret, ng, layer, w_o, w_up, w_down, tr):
    B, S, _ = p_ret.shape
    n_s = S // tr
    steps = B * n_s
    col = lambda j: pl.BlockSpec((1, tr, D_RET), lambda b, i, j=j: (b, i, j))

    def slab(w, n_slabs):
        rows = w.shape[1] // n_slabs
        assert rows * n_slabs == w.shape[1] and rows % 16 == 0 and steps % n_slabs == 0
        rep = steps // n_slabs
        return pl.BlockSpec((1, rows) + w.shape[2:],
                            lambda b, i: (layer, (b * n_s + i) // rep, 0))

    def slab_out(w, n_slabs):
        rows = w.shape[1] // n_slabs
        rep = steps // n_slabs
        return pl.BlockSpec((rows,) + w.shape[2:], lambda b, i: ((b * n_s + i) // rep, 0))

    slabs = ((w_o, steps), (w_up, steps), (w_down, steps // 2))
    return pl.pallas_call(
        _ret_kernel,
        out_shape=(jax.ShapeDtypeStruct((B, S, D_RET), BF16),)
        + tuple(jax.ShapeDtypeStruct(w.shape[1:], BF16) for w, _ in slabs),
        grid=(B, n_s),
        in_specs=[col(0), col(1), col(2), col(3),
                  pl.BlockSpec((1, D_RET), lambda b, i: (0, 0))]
        + [slab(w, n) for w, n in slabs],
        out_specs=(pl.BlockSpec((1, tr, D_RET), lambda b, i: (b, i, 0)),)
        + tuple(slab_out(w, n) for w, n in slabs),
        scratch_shapes=[pltpu.VMEM((N_RET_HEADS // 2, LANES, LANES), F32)],
        compiler_params=_cparams("arbitrary", "arbitrary"),
        name="retention",
    )(p_ret, p_ret, p_ret, p_ret, ng, w_o, w_up, w_down)


def _fox_kernel(q_ref, k_ref, vt_ref, o_ref, m_ref, acc_ref, s0_ref, s1_ref, mb0_ref, mb1_ref):
    s_refs = (s0_ref, s1_ref)
    mb_refs = (mb0_ref, mb1_ref)
    i = pl.program_id(2)
    n_qt = FOX_BQ // FOX_TQ
    units = [(hh, qt) for hh in range(2) for qt in range(n_qt)]
    n_u = len(units)
    dn_t = (((1,), (1,)), ((), ()))
    m_ref[...] = jnp.full(m_ref.shape, NEG_INF, F32)
    acc_ref[...] = jnp.zeros_like(acc_ref)

    def diag_keys(qt, half):
        d = qt * FOX_TQ - half * FOX_TK
        nk = min(max(d + FOX_TQ, 0), FOX_TK)
        return nk, (d if nk - 1 > d else None)

    def scores(j, slot, u):
        hh, qt = units[u]
        start = pl.multiple_of(j * FOX_TK, FOX_TK)
        k = k_ref[0, hh, pl.ds(start, FOX_TK), :]
        q = q_ref[0, hh, qt * FOX_TQ:(qt + 1) * FOX_TQ, :]
        s = lax.dot_general(k, q, dn_t, preferred_element_type=F32)
        s_refs[slot][u] = s
        mb_refs[slot][u] = jnp.max(s, axis=0, keepdims=True)

    def update(j, slot, u, diag_half=None):
        hh, qt = units[u]
        nk, mask_off = (FOX_TK, None) if diag_half is None else diag_keys(qt, diag_half)
        if nk == 0:
            return
        s = s_refs[slot][u, 0:nk, :]
        m_blk = mb_refs[slot][u]
        if mask_off is not None or nk < FOX_TK:
            if mask_off is not None:
                key = lax.broadcasted_iota(jnp.int32, s.shape, 0)
                qry = lax.broadcasted_iota(jnp.int32, s.shape, 1)
                s = jnp.where(key <= qry + mask_off, s, NEG_INF)
            m_blk = jnp.max(s, axis=0, keepdims=True)
        m_old = m_ref[hh, qt]
        m_new = jnp.maximum(m_old, m_blk)
        p = jnp.exp2((s - m_new).astype(BF16))
        vt = vt_ref[0, hh, j, :, 0:nk]
        acc_ref[hh, qt] = jnp.exp2(m_old - m_new) * acc_ref[hh, qt] + jnp.dot(
            vt, p, preferred_element_type=F32)
        m_ref[hh, qt] = m_new

    for u in range(n_u):
        scores(0, 0, u)

    def body(j, carry):
        for u in range(n_u):
            scores(2 * j + 1, 1, u)
            update(2 * j, 0, u)
        for u in range(n_u):
            scores(2 * j + 2, 0, u)
            update(2 * j + 1, 1, u)
        return carry

    lax.fori_loop(0, i, body, 0)
    for u in range(n_u):
        if diag_keys(units[u][1], 1)[0] > 0:
            scores(2 * i + 1, 1, u)
        update(2 * i, 0, u, 0)
    for u in range(n_u):
        update(2 * i + 1, 1, u, 1)

    for qt in range(n_qt):
        rows = []
        for hh in range(2):
            acc = acc_ref[hh, qt]
            rows.append(acc[0:HEAD_DIM] * (1.0 / acc[HEAD_DIM:HEAD_DIM + 1]))
        pair_t = jnp.concatenate(rows, axis=0)
        o_ref[0, qt * FOX_TQ:(qt + 1) * FOX_TQ, :] = pair_t.T.astype(BF16)


def _fox_attention(qa, ka, vt):
    B, H, S, _ = qa.shape
    n_qt = FOX_BQ // FOX_TQ
    stage = pltpu.VMEM((2 * n_qt, FOX_TK, FOX_TQ), F32)
    stage_max = pltpu.VMEM((2 * n_qt, 1, FOX_TQ), F32)
    return pl.pallas_call(
        _fox_kernel,
        out_shape=jax.ShapeDtypeStruct((B, S, D_FOX), BF16),
        grid=(B, H // 2, S // FOX_BQ),
        in_specs=[pl.BlockSpec((1, 2, FOX_BQ, LANES), lambda b, p, i: (b, p, i, 0)),
                  pl.BlockSpec((1, 2, S, LANES), lambda b, p, i: (b, p, 0, 0)),
                  pl.BlockSpec((1, 2, S // FOX_TK, VT_ROWS, FOX_TK),
                               lambda b, p, i: (b, p, 0, 0, 0))],
        out_specs=pl.BlockSpec((1, FOX_BQ, LANES), lambda b, p, i: (b, i, p)),
        scratch_shapes=[pltpu.VMEM((2, n_qt, 1, FOX_TQ), F32),
                        pltpu.VMEM((2, n_qt, VT_ROWS, FOX_TQ), F32),
                        stage, stage, stage_max, stage_max],
        compiler_params=_cparams("arbitrary", "arbitrary", "arbitrary"),
        name="fox_attention",
    )(qa, ka, vt)


FF_CHUNK = MXU_DIM


def _mix_ffn_kernel(yr_ref, yg_ref, yf_ref, x_ref, gt1_ref, sc_ref, sh_ref, gt2_ref, g_ref,
                    wo_ref, wu_ref, cw_ref, cb_ref, wd_ref, o_ref, carry_ref, stage_ref, act_ref):
    tm = x_ref.shape[1]

    @pl.when(pl.program_id(1) == 0)
    def _():
        carry_ref[...] = jnp.zeros_like(carry_ref)

    halves = []
    for rs in (slice(0, tm // 2), slice(tm // 2, tm)):
        mix = jnp.concatenate([yr_ref[0, rs], yg_ref[0, rs], yf_ref[0, rs]], axis=-1)
        xn = x_ref[0, rs] + gt1_ref[0] * jnp.dot(mix, wo_ref[...], preferred_element_type=F32)
        o_ref[0, rs] = xn
        halves.append(xn)
    h = jnp.concatenate([_mod_rms(xn, g_ref[...], sc_ref[0], sh_ref[0]).astype(BF16)
                         for xn in halves], axis=0)

    def conv_cols(c0):
        cs = slice(c0, c0 + FF_CHUNK)
        up = jnp.dot(h, wu_ref[:, cs], preferred_element_type=F32)
        stage_ref[0:SUBLANES, :] = carry_ref[:, cs]
        stage_ref[SUBLANES:, :] = up
        carry_ref[:, cs] = up[tm - SUBLANES:, :]
        prev1 = stage_ref[SUBLANES - 1:SUBLANES - 1 + tm, :]
        prev2 = stage_ref[SUBLANES - 2:SUBLANES - 2 + tm, :]
        return (cw_ref[0, 2:3, cs] * up + cw_ref[0, 1:2, cs] * prev1 + cw_ref[0, 0:1, cs] * prev2
                + cb_ref[0, :, cs])

    for f in range(D_FF // FF_CHUNK):
        a = conv_cols(f * FF_CHUNK)
        b = conv_cols(D_FF + f * FF_CHUNK)
        act_ref[:, f * FF_CHUNK:(f + 1) * FF_CHUNK] = ((a * jax.nn.sigmoid(a)) * b).astype(BF16)
    down = jnp.dot(act_ref[...], wd_ref[...], preferred_element_type=F32)
    o_ref[0] = o_ref[0] + gt2_ref[0] * down


def _mix_ffn(yr, yg, yf, x, gt1, sc, sh, gt2, g, layer, w_o, w_up, conv_w, conv_b, w_down, tm):
    B, S, D = x.shape
    row = lambda n: pl.BlockSpec((1, tm, n), lambda b, i: (b, i, 0))
    vec = pl.BlockSpec((1, 1, D), lambda b, i: (b, 0, 0))
    return pl.pallas_call(
        _mix_ffn_kernel,
        out_shape=jax.ShapeDtypeStruct((B, S, D), F32),
        grid=(B, S // tm),
        in_specs=[row(D_RET), row(D_GM), row(D_FOX), row(D), vec, vec, vec, vec,
                  pl.BlockSpec((1, D), lambda b, i: (0, 0)),
                  _resident(w_o), _resident(w_up),
                  _layer_resident(conv_w, layer), _layer_resident(conv_b, layer),
                  _resident(w_down)],
        out_specs=row(D),
        scratch_shapes=[pltpu.VMEM((SUBLANES, 2 * D_FF), F32),
                        pltpu.VMEM((tm + SUBLANES, FF_CHUNK), F32),
                        pltpu.VMEM((tm, D_FF), BF16)],
        compiler_params=_cparams("arbitrary", "arbitrary"),
        name="mix_ffn",
    )(yr, yg, yf, x, gt1, sc, sh, gt2, g, w_o, w_up, conv_w, conv_b, w_down)


def kernel(x, c, ada_w, ada_b, norm1_g, w_in, ret_norm_g, gm_ln_g, gm_ln_b, gm_ws, gm_bs,
           fox_qn_g, fox_kn_g, fox_bf, w_o, norm2_g, w_up, conv_w, conv_b, w_down):
    B, S, D = x.shape
    L = ada_w.shape[0]
    assert D == D_MODEL and S % FOX_BQ == 0
    tm_proj = 512
    tm = 1024

    mod = _modulation(c, ada_w, ada_b).reshape(L, B, 6, 1, D)
    cos_t, sin_t, dq, dk = _rope_tables(S)
    w_in_b = w_in.astype(BF16)
    conv_b3 = conv_b.reshape(L, 1, 2 * D_FF)

    for l in range(L):
        sh1, sc1, gt1, sh2, sc2, gt2 = (mod[l, :, i] for i in range(6))
        ws_cat = gm_ws[l].transpose(1, 0, 2).reshape(CHUNK, N_GM_HEADS * CHUNK)
        gate_bias = jnp.repeat(gm_bs[l].T, HEAD_DIM, axis=1)
        bf_row = jnp.pad(fox_bf[l], (FL_LANE0, 0)).reshape(1, LANES)
        p_ret, y_g, qa, ka, vt = _in_projection(
            x, sc1, sh1, norm1_g[l].reshape(1, D), w_in_b, l, cos_t, sin_t, dq, dk,
            gm_ln_g[l].reshape(1, D_GM), gm_ln_b[l].reshape(1, D_GM), ws_cat, gate_bias, bf_row,
            jnp.tile(fox_qn_g[l], N_FOX_HEADS).reshape(1, D_FOX),
            jnp.tile(fox_kn_g[l], N_FOX_HEADS).reshape(1, D_FOX), tm_proj)

        y_r, w_o_b, w_up_b, w_down_b = _retention(p_ret, ret_norm_g[l].reshape(1, D_RET), l,
                                                   w_o, w_up, w_down, tm)
        y_f = _fox_attention(qa, ka, vt)

        x = _mix_ffn(y_r, y_g, y_f, x, gt1, sc2, sh2, gt2, norm2_g[l].reshape(1, D), l,
                     w_o_b, w_up_b, conv_w, conv_b3, w_down_b, tm)
    return x
```

```python
import functools
import math

import jax
import jax.numpy as jnp
from jax import lax
from jax.experimental import pallas as pl
from jax.experimental.pallas import tpu as pltpu

D_MODEL = 1024
HEAD_DIM = 64
N_RET_HEADS = 6
N_GM_HEADS = 4
N_FOX_HEADS = 6
D_RET = N_RET_HEADS * HEAD_DIM
D_GM = N_GM_HEADS * HEAD_DIM
D_FOX = N_FOX_HEADS * HEAD_DIM
CHUNK = 128
D_FF = 2816
ROPE_BASE = 10000.0
EPS = 1e-6
NEG_INF = -1e30

LANES = 128
SUBLANES = 8
MXU_DIM = 256
VMEM_LIMIT = 52 * 1024 * 1024

N_MAIN = 4 * D_RET + 2 * D_GM + 3 * D_FOX
GM_COL0 = 4 * D_RET
FOX_COL0 = GM_COL0 + 2 * D_GM
N_IN = N_MAIN + N_FOX_HEADS
FL_LANE0 = LANES - N_FOX_HEADS

LOG2E = math.log2(math.e)
FOX_TK = 1024
FOX_TQ = MXU_DIM
FOX_BQ = 2 * FOX_TK
VT_ROWS = 80

F32 = jnp.float32
BF16 = jnp.bfloat16


def _cparams(*sem):
    return pltpu.CompilerParams(dimension_semantics=sem, vmem_limit_bytes=VMEM_LIMIT)


def _resident(whole):
    nd = whole.ndim
    return pl.BlockSpec(whole.shape, lambda *_: (0,) * nd, pipeline_mode=pl.Buffered(1))


def _layer_resident(stacked, layer):
    nd = stacked.ndim - 1
    return pl.BlockSpec((1,) + stacked.shape[1:], lambda *_: (layer,) + (0,) * nd,
                        pipeline_mode=pl.Buffered(1))


def _lane_iota(shape):
    return lax.broadcasted_iota(jnp.int32, shape, len(shape) - 1)


def _mod_kernel(ct_ref, w_ref, b_ref, o_ref):
    ct = ct_ref[...]
    cond_t = ct * jax.nn.sigmoid(ct)
    w = w_ref[0]
    rows = [jnp.sum(cond_t[:, b:b + 1] * w, axis=0, keepdims=True)
            for b in range(ct.shape[1])]
    o_ref[0] = jnp.concatenate(rows, axis=0) + b_ref[0]


def _modulation(c, ada_w, ada_b):
    L, D, N = ada_w.shape
    B = c.shape[0]
    tn = 1536
    return pl.pallas_call(
        _mod_kernel,
        out_shape=jax.ShapeDtypeStruct((L, B, N), F32),
        grid=(L, N // tn),
        in_specs=[pl.BlockSpec((D, B), lambda l, j: (0, 0)),
                  pl.BlockSpec((1, D, tn), lambda l, j: (l, 0, j)),
                  pl.BlockSpec((1, 1, tn), lambda l, j: (l, 0, j))],
        out_specs=pl.BlockSpec((1, B, tn), lambda l, j: (l, 0, j)),
        compiler_params=_cparams("arbitrary", "arbitrary"),
        name="adaln_mod",
    )(c.T, ada_w, ada_b.reshape(L, 1, N))


def _ret_log_gamma(h):
    return math.log(1.0 - 2.0 ** (-5.0 - h))


def _rope_kernel(inv_ref, cos_ref, sin_ref, dq_ref, dk_ref):
    ts = cos_ref.shape[0]
    pos = (lax.broadcasted_iota(jnp.int32, (ts, LANES), 0) + pl.program_id(0) * ts).astype(F32)
    ang = pos * inv_ref[...]
    first_half = (_lane_iota((ts, LANES)) % HEAD_DIM) < (HEAD_DIM // 2)
    cos_ref[...] = jnp.cos(ang)
    s = jnp.sin(ang)
    sin_ref[...] = jnp.where(first_half, -s, s)

    head = _lane_iota((CHUNK, D_RET)) // HEAD_DIM
    lg = jnp.zeros((CHUNK, D_RET), F32)
    for h in range(N_RET_HEADS):
        lg = jnp.where(head == h, _ret_log_gamma(h), lg)
    t1 = lax.broadcasted_iota(jnp.int32, (CHUNK, D_RET), 0).astype(F32) + 1.0
    dq_ref[...] = jnp.exp(lg * t1)
    dk_ref[...] = jnp.exp(-lg * t1) * (HEAD_DIM ** -0.5)


def _rope_tables(S):
    half = HEAD_DIM // 2
    inv = ROPE_BASE ** (-jnp.arange(half, dtype=F32) / half)
    inv_row = jnp.tile(inv, LANES // half).reshape(1, LANES)
    ts = 512
    decay = jax.ShapeDtypeStruct((CHUNK, D_RET), F32)
    decay_spec = pl.BlockSpec((CHUNK, D_RET), lambda i: (0, 0))
    return pl.pallas_call(
        _rope_kernel,
        out_shape=(jax.ShapeDtypeStruct((S, LANES), F32),) * 2 + (decay, decay),
        grid=(S // ts,),
        in_specs=[pl.BlockSpec((1, LANES), lambda i: (0, 0))],
        out_specs=(pl.BlockSpec((ts, LANES), lambda i: (i, 0)),) * 2 + (decay_spec, decay_spec),
        compiler_params=_cparams("arbitrary"),
        name="rope_tables",
    )(inv_row)


def _mod_rms(x, g, sc, sh):
    ms = jnp.mean(x * x, axis=-1, keepdims=True)
    return (x * lax.rsqrt(ms + EPS) * g) * (1.0 + sc) + sh


def _swap_half_heads(x):
    first_half = (_lane_iota(x.shape) % HEAD_DIM) < (HEAD_DIM // 2)
    return jnp.where(first_half, pltpu.roll(x, LANES - HEAD_DIM // 2, 1),
                     pltpu.roll(x, HEAD_DIM // 2, 1))


def _gelu(x):
    return 0.5 * x * (1.0 + jnp.tanh(math.sqrt(2.0 / math.pi) * (x + 0.044715 * (x * x * x))))


def _pair_rms_scale(x):
    low = _lane_iota(x.shape) < HEAD_DIM
    sq = x * x
    ss0 = jnp.sum(jnp.where(low, sq, 0.0), axis=-1, keepdims=True)
    ss1 = jnp.sum(jnp.where(low, 0.0, sq), axis=-1, keepdims=True)
    return jnp.where(low, lax.rsqrt(ss0 * (1.0 / HEAD_DIM) + EPS),
                     lax.rsqrt(ss1 * (1.0 / HEAD_DIM) + EPS))


def _inproj_kernel(x_ref, xn_ref, sc_ref, sh_ref, g_ref, w_ref, cos_ref, sin_ref, dq_ref, dk_ref,
                   lng_ref, lnb_ref, ws_ref, gb_ref, bf_ref, qg_ref, kg_ref,
                   pret_ref, yg_ref, qa_ref, ka_ref, vt_ref,
                   h_a, h_b, nc_a, nc_b, carry_ref):
    tm = x_ref.shape[1]
    step = pl.program_id(1)

    def prepare(x_tile_ref, h_out, nc_out):
        stages, box = [], {}

        def norm_rows(q):
            rs = slice(q * CHUNK, (q + 1) * CHUNK)
            h_out[rs, :] = _mod_rms(x_tile_ref[0, rs], g_ref[...], sc_ref[0], sh_ref[0]).astype(BF16)

        def forget_logits():
            z = jnp.dot(h_out[...], w_ref[0, :, N_IN - LANES:N_IN],
                        preferred_element_type=F32) + bf_ref[...]
            box["logf"] = jnp.minimum(z, 0.0) - jnp.log1p(jnp.exp(-jnp.abs(z)))

        def cumulate():
            tri = (lax.broadcasted_iota(jnp.int32, (CHUNK, CHUNK), 1) <=
                   lax.broadcasted_iota(jnp.int32, (CHUNK, CHUNK), 0)).astype(F32)
            logf = box["logf"]
            within = [jnp.dot(tri, logf[c * CHUNK:(c + 1) * CHUNK],
                              precision=lax.Precision.HIGHEST, preferred_element_type=F32)
                      for c in range(tm // CHUNK)]
            carry = carry_ref[0:1, :]
            for c, cum in enumerate(within):
                nc_out[c * CHUNK:(c + 1) * CHUNK, :] = (cum + carry) * (-LOG2E)
                carry = carry + cum[CHUNK - 1:CHUNK, :]
            carry_ref[...] = jnp.broadcast_to(carry, carry_ref.shape)

        for q in range(tm // CHUNK):
            stages.append(functools.partial(norm_rows, q))
        return stages + [forget_logits, cumulate]

    @pl.when(step == 0)
    def _():
        carry_ref[...] = jnp.zeros_like(carry_ref)
        for stage in prepare(x_ref, h_a, nc_a):
            stage()

    @pl.when(step % 2 == 0)
    def _():
        _inproj_tile(h_a, nc_a, prepare(xn_ref, h_b, nc_b), w_ref, cos_ref, sin_ref,
                     dq_ref, dk_ref, lng_ref, lnb_ref, ws_ref, gb_ref, qg_ref, kg_ref,
                     pret_ref, yg_ref, qa_ref, ka_ref, vt_ref)

    @pl.when(step % 2 == 1)
    def _():
        _inproj_tile(h_b, nc_b, prepare(xn_ref, h_a, nc_a), w_ref, cos_ref, sin_ref,
                     dq_ref, dk_ref, lng_ref, lnb_ref, ws_ref, gb_ref, qg_ref, kg_ref,
                     pret_ref, yg_ref, qa_ref, ka_ref, vt_ref)


def _inproj_tile(h_ref, nc_ref, next_stages, w_ref, cos_ref, sin_ref, dq_ref, dk_ref, lng_ref,
                 lnb_ref, ws_ref, gb_ref, qg_ref, kg_ref, pret_ref, yg_ref, qa_ref, ka_ref, vt_ref):
    tm = h_ref.shape[0]
    h = h_ref[...]
    neg_cum = nc_ref[...]
    bias_hi = neg_cum.astype(BF16).astype(F32)
    rest = neg_cum - bias_hi
    bias_mid = rest.astype(BF16).astype(F32)
    bias_lo = (rest - bias_mid).astype(BF16).astype(F32)

    lane = _lane_iota((tm, LANES))
    low = lane < HEAD_DIM
    ones_row = jnp.where(lax.broadcasted_iota(jnp.int32, (VT_ROWS - HEAD_DIM, tm), 0) == 0,
                         1.0, 0.0).astype(BF16)

    def fox_tiles(piece, kind, pair):
        if kind == 2:
            v_t = piece.T.astype(BF16)
        else:
            cols = slice(pair * LANES, (pair + 1) * LANES)
            gain = qg_ref[:, cols] * (HEAD_DIM ** -0.5 * LOG2E) if kind == 0 else kg_ref[:, cols]
            normed = piece * _pair_rms_scale(piece) * gain
        for half in range(2):
            hd = 2 * pair + half
            own = low if half == 0 else jnp.logical_not(low)
            b0 = HEAD_DIM * (1 - half)
            if kind == 0:
                ones = jnp.where((lane >= b0) & (lane < b0 + 3), 1.0, 0.0)
                qa_ref[0, hd] = jnp.where(own, normed, ones).astype(BF16)
            elif kind == 1:
                col = slice(FL_LANE0 + hd, FL_LANE0 + hd + 1)
                bias = jnp.where(lane == b0, bias_hi[:, col],
                                 jnp.where(lane == b0 + 1, bias_mid[:, col],
                                           jnp.where(lane == b0 + 2, bias_lo[:, col], 0.0)))
                ka_ref[0, hd] = jnp.where(own, normed, bias).astype(BF16)
            else:
                vt_ref[0, hd, 0, 0:HEAD_DIM, :] = v_t[half * HEAD_DIM:(half + 1) * HEAD_DIM]
                vt_ref[0, hd, 0, HEAD_DIM:VT_ROWS, :] = ones_row

    def spatial_gate(u, vn):
        lane_c = _lane_iota((CHUNK, LANES))
        keep_a = jnp.where(lane_c < HEAD_DIM, 1.0, 0.0).astype(BF16)
        keep_b = jnp.where(lane_c < HEAD_DIM, 0.0, 1.0).astype(BF16)
        row = lax.broadcasted_iota(jnp.int32, (CHUNK, 2 * CHUNK), 0)
        col = _lane_iota((CHUNK, 2 * CHUNK)) % CHUNK
        vb = vn.astype(BF16)
        for pair in range(N_GM_HEADS // 2):
            w_pair = jnp.where(col <= row, ws_ref[:, pair * 2 * CHUNK:(pair + 1) * 2 * CHUNK],
                               0.0).astype(BF16)
            ls = slice(pair * LANES, (pair + 1) * LANES)
            for c in range(tm // CHUNK):
                rs = slice(c * CHUNK, (c + 1) * CHUNK)
                v_tile = vb[rs, ls]
                stacked = jnp.concatenate([v_tile * keep_a, v_tile * keep_b], axis=0)
                mixed = jnp.dot(w_pair, stacked, preferred_element_type=F32)
                yg_ref[0, rs, ls] = (u[rs, ls] * (mixed + gb_ref[:, ls])).astype(BF16)

    def rotary_decay(piece, table_ref, tile):
        rot = piece * cos_ref[...] + _swap_half_heads(piece) * sin_ref[...]
        table = table_ref[:, tile * LANES:(tile + 1) * LANES]
        return rot * jnp.concatenate([table] * (tm // CHUNK), axis=0)

    for lo in range(0, N_MAIN, MXU_DIM):
        width = min(MXU_DIM, N_MAIN - lo)
        r = jnp.dot(h, w_ref[0, :, lo:lo + width], preferred_element_type=F32)
        if next_stages:
            next_stages.pop(0)()
        if lo == GM_COL0:
            gate_u = _gelu(r)
            continue
        if lo == GM_COL0 + D_GM:
            v = _gelu(r)
            mu = jnp.mean(v, axis=-1, keepdims=True)
            vc = v - mu
            var = jnp.mean(vc * vc, axis=-1, keepdims=True)
            spatial_gate(gate_u, vc * lax.rsqrt(var + EPS) * lng_ref[...] + lnb_ref[...])
            continue
        for off in range(0, width, LANES):
            c0 = lo + off
            piece = r[:, off:off + LANES]
            if c0 < D_RET:
                piece = rotary_decay(piece, dq_ref, c0 // LANES)
            elif c0 < 2 * D_RET:
                piece = rotary_decay(piece, dk_ref, (c0 - D_RET) // LANES)
            elif 3 * D_RET <= c0 < 4 * D_RET:
                piece = piece * jax.nn.sigmoid(piece)
            if c0 < GM_COL0:
                pret_ref[0, :, c0:c0 + LANES] = piece.astype(BF16)
            else:
                kind, pair = divmod((c0 - FOX_COL0) // LANES, N_FOX_HEADS // 2)
                fox_tiles(piece, kind, pair)


def _in_projection(x, sc, sh, g, w_in, layer, cos_t, sin_t, dq, dk, ln_g, ln_b, ws_cat,
                   gate_bias, bf_row, qg, kg, tm):
    B, S, D = x.shape
    assert FOX_TK % tm == 0 and (S // tm) % 2 == 0
    per_kb = FOX_TK // tm
    n_s = S // tm
    const = lambda shape: pl.BlockSpec(shape, lambda b, i: (0,) * len(shape))
    aug = jax.ShapeDtypeStruct((B, N_FOX_HEADS, S, LANES), BF16)
    aug_spec = pl.BlockSpec((1, N_FOX_HEADS, tm, LANES), lambda b, i: (b, 0, i, 0))
    return pl.pallas_call(
        _inproj_kernel,
        out_shape=(jax.ShapeDtypeStruct((B, S, 4 * D_RET), BF16),
                   jax.ShapeDtypeStruct((B, S, D_GM), BF16),
                   aug, aug,
                   jax.ShapeDtypeStruct((B, N_FOX_HEADS, S // FOX_TK, VT_ROWS, FOX_TK), BF16)),
        grid=(B, S // tm),
        in_specs=[pl.BlockSpec((1, tm, D), lambda b, i: (b, i, 0)),
                  pl.BlockSpec((1, tm, D), lambda b, i: (b, jnp.minimum(i + 1, n_s - 1), 0)),
                  pl.BlockSpec((1, 1, D), lambda b, i: (b, 0, 0)),
                  pl.BlockSpec((1, 1, D), lambda b, i: (b, 0, 0)),
                  const((1, D)),
                  _layer_resident(w_in, layer),
                  pl.BlockSpec((tm, LANES), lambda b, i: (i, 0)),
                  pl.BlockSpec((tm, LANES), lambda b, i: (i, 0)),
                  const((CHUNK, D_RET)), const((CHUNK, D_RET)),
                  const((1, D_GM)), const((1, D_GM)),
                  const((CHUNK, N_GM_HEADS * CHUNK)), const((CHUNK, D_GM)),
                  const((1, LANES)), const((1, D_FOX)), const((1, D_FOX))],
        out_specs=(pl.BlockSpec((1, tm, 4 * D_RET), lambda b, i: (b, i, 0)),
                   pl.BlockSpec((1, tm, D_GM), lambda b, i: (b, i, 0)),
                   aug_spec, aug_spec,
                   pl.BlockSpec((1, N_FOX_HEADS, 1, VT_ROWS, tm),
                                lambda b, i: (b, 0, i // per_kb, 0, i % per_kb))),
        scratch_shapes=[pltpu.VMEM((tm, D), BF16), pltpu.VMEM((tm, D), BF16),
                        pltpu.VMEM((tm, LANES), F32), pltpu.VMEM((tm, LANES), F32),
                        pltpu.VMEM((SUBLANES, LANES), F32)],
        compiler_params=_cparams("arbitrary", "arbitrary"),
        name="in_proj",
    )(x, x, sc, sh, g, w_in, cos_t, sin_t, dq, dk, ln_g, ln_b, ws_cat, gate_bias, bf_row, qg, kg)


def _ret_kernel(q_ref, k_ref, v_ref, g_ref, ng_ref, wo_ref, wu_ref, wd_ref,
                o_ref, wo_out, wu_out, wd_out, state_ref):
    wo_out[...] = wo_ref[0].astype(BF16)
    wu_out[...] = wu_ref[0].astype(BF16)
    wd_out[...] = wd_ref[0].astype(BF16)

    tr = q_ref.shape[1]
    n_pairs = N_RET_HEADS // 2
    low = _lane_iota((CHUNK, LANES)) < HEAD_DIM
    keep0 = jnp.where(low, 1.0, 0.0).astype(BF16)
    keep1 = jnp.where(low, 0.0, 1.0).astype(BF16)
    causal = (lax.broadcasted_iota(jnp.int32, (CHUNK, CHUNK), 0) >=
              lax.broadcasted_iota(jnp.int32, (CHUNK, CHUNK), 1))
    blockdiag = (lax.broadcasted_iota(jnp.int32, (LANES, LANES), 0) < HEAD_DIM) == low
    dn_t = (((1,), (1,)), ((), ()))
    dn_ta = (((0,), (0,)), ((), ()))

    @pl.when(pl.program_id(1) == 0)
    def _():
        state_ref[...] = jnp.zeros_like(state_ref)

    n_c = tr // CHUNK
    tiles = [(p, c) for p in range(n_pairs) for c in range(n_c)]
    sl = lambda p, c: (slice(c * CHUNK, (c + 1) * CHUNK), slice(p * LANES, (p + 1) * LANES))
    kvs, scores = {}, {}
    for p, c in tiles:
        rs, cs = sl(p, c)
        q, k, v = q_ref[0, rs, cs], k_ref[0, rs, cs], v_ref[0, rs, cs]
        kvs[p, c] = lax.dot_general(k, v, dn_ta, preferred_element_type=F32)
        for hh, keep in enumerate((keep0, keep1)):
            s = lax.dot_general(q * keep, k, dn_t, preferred_element_type=F32)
            scores[p, c, hh] = jnp.where(causal, s, 0.0).astype(BF16)
    states = {}
    for p in range(n_pairs):
        chunk_decay = jnp.where(low[0:1, :], math.exp(_ret_log_gamma(2 * p) * CHUNK),
                                math.exp(_ret_log_gamma(2 * p + 1) * CHUNK))
        state = state_ref[p]
        for c in range(n_c):
            states[p, c] = state.astype(BF16)
            state = (state + jnp.where(blockdiag, kvs[p, c], 0.0)) * chunk_decay
        state_ref[p] = state
    for p, c in tiles:
        rs, cs = sl(p, c)
        q = q_ref[0, rs, cs]
        rhs = jnp.concatenate([v_ref[0, rs, cs], states[p, c]], axis=0)
        ys = [jnp.dot(jnp.concatenate([scores[p, c, hh], q * keep], axis=1), rhs,
                      preferred_element_type=F32) for hh, keep in enumerate((keep0, keep1))]
        y = jnp.where(low, ys[0], ys[1])
        yn = y * _pair_rms_scale(y) * ng_ref[:, cs]
        o_ref[0, rs, cs] = (yn * g_ref[0, rs, cs].astype(F32)).astype(BF16)


def _retention(p_ret, ng, layer, w_o, w_up, w_down, tr):
    B, S, _ = p_ret.shape
    n_s = S // tr
    steps = B * n_s
    col = lambda j: pl.BlockSpec((1, tr, D_RET), lambda b, i, j=j: (b, i, j))

    def slab(w, n_slabs):
        rows = w.shape[1] // n_slabs
        assert rows * n_slabs == w.shape[1] and rows % 16 == 0 and steps % n_slabs == 0
        rep = steps // n_slabs
        return pl.BlockSpec((1, rows) + w.shape[2:],
                            lambda b, i: (layer, (b * n_s + i) // rep, 0))

    def slab_out(w, n_slabs):
        rows = w.shape[1] // n_slabs
        rep = steps // n_slabs
        return pl.BlockSpec((rows,) + w.shape[2:], lambda b, i: ((b * n_s + i) // rep, 0))

    slabs = ((w_o, steps), (w_up, steps), (w_down, steps // 2))
    return pl.pallas_call(
        _ret_kernel,
        out_shape=(jax.ShapeDtypeStruct((B, S, D_RET), BF16),)
        + tuple(jax.ShapeDtypeStruct(w.shape[1:], BF16) for w, _ in slabs),
        grid=(B, n_s),
        in_specs=[col(0), col(1), col(2), col(3),
                  pl.BlockSpec((1, D_RET), lambda b, i: (0, 0))]
        + [slab(w, n) for w, n in slabs],
        out_specs=(pl.BlockSpec((1, tr, D_RET), lambda b, i: (b, i, 0)),)
        + tuple(slab_out(w, n) for w, n in slabs),
        scratch_shapes=[pltpu.VMEM((N_RET_HEADS // 2, LANES, LANES), F32)],
        compiler_params=_cparams("arbitrary", "arbitrary"),
        name="retention",
    )(p_ret, p_ret, p_ret, p_ret, ng, w_o, w_up, w_down)


def _fox_kernel(q_ref, k_ref, vt_ref, o_ref, m_ref, acc_ref, s0_ref, s1_ref, mb0_ref, mb1_ref):
    s_refs = (s0_ref, s1_ref)
    mb_refs = (mb0_ref, mb1_ref)
    i = pl.program_id(2)
    n_qt = FOX_BQ // FOX_TQ
    units = [(hh, qt) for hh in range(2) for qt in range(n_qt)]
    n_u = len(units)
    dn_t = (((1,), (1,)), ((), ()))
    m_ref[...] = jnp.full(m_ref.shape, NEG_INF, F32)
    acc_ref[...] = jnp.zeros_like(acc_ref)

    def diag_keys(qt, half):
        d = qt * FOX_TQ - half * FOX_TK
        nk = min(max(d + FOX_TQ, 0), FOX_TK)
        return nk, (d if nk - 1 > d else None)

    def scores(j, slot, u, nk=FOX_TK):
        hh, qt = units[u]
        start = pl.multiple_of(j * FOX_TK, FOX_TK)
        k = k_ref[0, hh, pl.ds(start, nk), :]
        q = q_ref[0, hh, qt * FOX_TQ:(qt + 1) * FOX_TQ, :]
        s = lax.dot_general(k, q, dn_t, preferred_element_type=F32)
        s_refs[slot][u, 0:nk, :] = s
        mb_refs[slot][u] = jnp.max(s, axis=0, keepdims=True)

    def update(j, slot, u, diag_half=None):
        hh, qt = units[u]
        nk, mask_off = (FOX_TK, None) if diag_half is None else diag_keys(qt, diag_half)
        if nk == 0:
            return
        s = s_refs[slot][u, 0:nk, :]
        m_blk = mb_refs[slot][u]
        if mask_off is not None or nk < FOX_TK:
            if mask_off is not None:
                key = lax.broadcasted_iota(jnp.int32, s.shape, 0)
                qry = lax.broadcasted_iota(jnp.int32, s.shape, 1)
                s = jnp.where(key <= qry + mask_off, s, NEG_INF)
            m_blk = jnp.max(s, axis=0, keepdims=True)
        m_old = m_ref[hh, qt]
        m_new = jnp.maximum(m_old, m_blk)
        p = jnp.exp2((s - m_new).astype(BF16))
        vt = vt_ref[0, hh, j, :, 0:nk]
        acc_ref[hh, qt] = jnp.exp2(m_old - m_new) * acc_ref[hh, qt] + jnp.dot(
            vt, p, preferred_element_type=F32)
        m_ref[hh, qt] = m_new

    @pl.when(i == 0)
    def _():
        for u in range(n_u):
            scores(0, 0, u, diag_keys(units[u][1], 0)[0])

    @pl.when(i > 0)
    def _():
        for u in range(n_u):
            scores(0, 0, u)

    def body(j, carry):
        for u in range(n_u):
            scores(2 * j + 1, 1, u)
            update(2 * j, 0, u)
        for u in range(n_u):
            scores(2 * j + 2, 0, u)
            update(2 * j + 1, 1, u)
        return carry

    lax.fori_loop(0, i, body, 0)
    for u in range(n_u):
        if diag_keys(units[u][1], 1)[0] > 0:
            scores(2 * i + 1, 1, u, diag_keys(units[u][1], 1)[0])
        update(2 * i, 0, u, 0)
    for u in range(n_u):
        update(2 * i + 1, 1, u, 1)

    for qt in range(n_qt):
        rows = []
        for hh in range(2):
            acc = acc_ref[hh, qt]
            rows.append(acc[0:HEAD_DIM] * (1.0 / acc[HEAD_DIM:HEAD_DIM + 1]))
        pair_t = jnp.concatenate(rows, axis=0)
        o_ref[0, qt * FOX_TQ:(qt + 1) * FOX_TQ, :] = pair_t.T.astype(BF16)


def _fox_attention(qa, ka, vt):
    B, H, S, _ = qa.shape
    n_qt = FOX_BQ // FOX_TQ
    stage = pltpu.VMEM((2 * n_qt, FOX_TK, FOX_TQ), F32)
    stage_max = pltpu.VMEM((2 * n_qt, 1, FOX_TQ), F32)
    return pl.pallas_call(
        _fox_kernel,
        out_shape=jax.ShapeDtypeStruct((B, S, D_FOX), BF16),
        grid=(B, H // 2, S // FOX_BQ),
        in_specs=[pl.BlockSpec((1, 2, FOX_BQ, LANES), lambda b, p, i: (b, p, i, 0)),
                  pl.BlockSpec((1, 2, S, LANES), lambda b, p, i: (b, p, 0, 0)),
                  pl.BlockSpec((1, 2, S // FOX_TK, VT_ROWS, FOX_TK),
                               lambda b, p, i: (b, p, 0, 0, 0))],
        out_specs=pl.BlockSpec((1, FOX_BQ, LANES), lambda b, p, i: (b, i, p)),
        scratch_shapes=[pltpu.VMEM((2, n_qt, 1, FOX_TQ), F32),
                        pltpu.VMEM((2, n_qt, VT_ROWS, FOX_TQ), F32),
                        stage, stage, stage_max, stage_max],
        compiler_params=_cparams("arbitrary", "arbitrary", "arbitrary"),
        name="fox_attention",
    )(qa, ka, vt)


FF_CHUNK = MXU_DIM


def _mix_ffn_kernel(yr_ref, yg_ref, yf_ref, x_ref, gt1_ref, sc_ref, sh_ref, gt2_ref, g_ref,
                    wo_ref, wu_ref, cw_ref, cb_ref, wd_ref, o_ref, carry_ref, stage_ref, act_ref):
    tm = x_ref.shape[1]

    @pl.when(pl.program_id(1) == 0)
    def _():
        carry_ref[...] = jnp.zeros_like(carry_ref)

    halves = []
    for rs in (slice(0, tm // 2), slice(tm // 2, tm)):
        mix = jnp.concatenate([yr_ref[0, rs], yg_ref[0, rs], yf_ref[0, rs]], axis=-1)
        xn = x_ref[0, rs] + gt1_ref[0] * jnp.dot(mix, wo_ref[...], preferred_element_type=F32)
        o_ref[0, rs] = xn
        halves.append(xn)
    h = jnp.concatenate([_mod_rms(xn, g_ref[...], sc_ref[0], sh_ref[0]).astype(BF16)
                         for xn in halves], axis=0)

    def conv_cols(c0):
        cs = slice(c0, c0 + FF_CHUNK)
        up = jnp.dot(h, wu_ref[:, cs], preferred_element_type=F32)
        stage_ref[0:SUBLANES, :] = carry_ref[:, cs]
        stage_ref[SUBLANES:, :] = up
        carry_ref[:, cs] = up[tm - SUBLANES:, :]
        prev1 = stage_ref[SUBLANES - 1:SUBLANES - 1 + tm, :]
        prev2 = stage_ref[SUBLANES - 2:SUBLANES - 2 + tm, :]
        return (cw_ref[0, 2:3, cs] * up + cw_ref[0, 1:2, cs] * prev1 + cw_ref[0, 0:1, cs] * prev2
                + cb_ref[0, :, cs])

    for f in range(D_FF // FF_CHUNK):
        a = conv_cols(f * FF_CHUNK)
        b = conv_cols(D_FF + f * FF_CHUNK)
        act_ref[:, f * FF_CHUNK:(f + 1) * FF_CHUNK] = ((a * jax.nn.sigmoid(a)) * b).astype(BF16)
    down = jnp.dot(act_ref[...], wd_ref[...], preferred_element_type=F32)
    o_ref[0] = o_ref[0] + gt2_ref[0] * down


def _mix_ffn(yr, yg, yf, x, gt1, sc, sh, gt2, g, layer, w_o, w_up, conv_w, conv_b, w_down, tm):
    B, S, D = x.shape
    row = lambda n: pl.BlockSpec((1, tm, n), lambda b, i: (b, i, 0))
    vec = pl.BlockSpec((1, 1, D), lambda b, i: (b, 0, 0))
    return pl.pallas_call(
        _mix_ffn_kernel,
        out_shape=jax.ShapeDtypeStruct((B, S, D), F32),
        grid=(B, S // tm),
        in_specs=[row(D_RET), row(D_GM), row(D_FOX), row(D), vec, vec, vec, vec,
                  pl.BlockSpec((1, D), lambda b, i: (0, 0)),
                  _resident(w_o), _resident(w_up),
                  _layer_resident(conv_w, layer), _layer_resident(conv_b, layer),
                  _resident(w_down)],
        out_specs=row(D),
        scratch_shapes=[pltpu.VMEM((SUBLANES, 2 * D_FF), F32),
                        pltpu.VMEM((tm + SUBLANES, FF_CHUNK), F32),
                        pltpu.VMEM((tm, D_FF), BF16)],
        compiler_params=_cparams("arbitrary", "arbitrary"),
        name="mix_ffn",
    )(yr, yg, yf, x, gt1, sc, sh, gt2, g, w_o, w_up, conv_w, conv_b, w_down)


def kernel(x, c, ada_w, ada_b, norm1_g, w_in, ret_norm_g, gm_ln_g, gm_ln_b, gm_ws, gm_bs,
           fox_qn_g, fox_kn_g, fox_bf, w_o, norm2_g, w_up, conv_w, conv_b, w_down):
    B, S, D = x.shape
    L = ada_w.shape[0]
    assert D == D_MODEL and S % FOX_BQ == 0
    tm_proj = 512
    tm = 1024

    mod = _modulation(c, ada_w, ada_b).reshape(L, B, 6, 1, D)
    cos_t, sin_t, dq, dk = _rope_tables(S)
    w_in_b = w_in.astype(BF16)
    conv_b3 = conv_b.reshape(L, 1, 2 * D_FF)

    for l in range(L):
        sh1, sc1, gt1, sh2, sc2, gt2 = (mod[l, :, i] for i in range(6))
        ws_cat = gm_ws[l].transpose(1, 0, 2).reshape(CHUNK, N_GM_HEADS * CHUNK)
        gate_bias = jnp.repeat(gm_bs[l].T, HEAD_DIM, axis=1)
        bf_row = jnp.pad(fox_bf[l], (FL_LANE0, 0)).reshape(1, LANES)
        p_ret, y_g, qa, ka, vt = _in_projection(
            x, sc1, sh1, norm1_g[l].reshape(1, D), w_in_b, l, cos_t, sin_t, dq, dk,
            gm_ln_g[l].reshape(1, D_GM), gm_ln_b[l].reshape(1, D_GM), ws_cat, gate_bias, bf_row,
            jnp.tile(fox_qn_g[l], N_FOX_HEADS).reshape(1, D_FOX),
            jnp.tile(fox_kn_g[l], N_FOX_HEADS).reshape(1, D_FOX), tm_proj)

        y_r, w_o_b, w_up_b, w_down_b = _retention(p_ret, ret_norm_g[l].reshape(1, D_RET), l,
                                                   w_o, w_up, w_down, tm)
        y_f = _fox_attention(qa, ka, vt)

        x = _mix_ffn(y_r, y_g, y_f, x, gt1, sc2, sh2, gt2, norm2_g[l].reshape(1, D), l,
                     w_o_b, w_up_b, conv_w, conv_b3, w_down_b, tm)
    return x
```
